```python
import math
import jax
import jax.numpy as jnp
from jax import lax
import numpy as np

D_MODEL = 2048
BATCH = 32
SEQ = 256
DEPTH = 2
DEC_BATCH = 4
DEC_SEQ = 4096
PAST_LEN = 256

GRID_W = 64
Q_BLOCK = 128
EPS = 1e-6
ROPE_BASE = 10000.0
NEG_INF = -1e30
H_A = 4
DH_A = 64
DV_A = 2 * DH_A
W_B = 512
CG_B = 16
G_B = W_B // CG_B
P_B = 64
H_C = 8
KVH_C = 2
DH_C = 64
WINDOW_C = 128
H_D = 8
DH_D = 64
WIN_R = 8
WIN_C = 16
BRANCH_W = 512
N_BRANCH = 4
N_EXPERTS = 32
N_GROUPS = 8
EXPERTS_PER_GROUP = N_EXPERTS // N_GROUPS
TOP_K = 2
D_EXPERT = 512
MOE_BLOCK = 128
IN_SIZES = (H_A * 2 * DH_A, H_A * 2 * DH_A, H_A * DV_A, W_B,
            H_C * DH_C, KVH_C * DH_C, KVH_C * DH_C,
            H_D * DH_D, H_D * DH_D, H_D * DH_D, N_BRANCH * D_MODEL)
N_IN = sum(IN_SIZES)

kernel_name = 'hybrid_diffusion_prefix_trunk_step'


def rmsnorm(x, g):
    xf = x.astype(jnp.float32)
    y = xf * lax.rsqrt(jnp.mean(xf * xf, axis=-1, keepdims=True) + EPS)
    return (y * g.astype(jnp.float32)).astype(x.dtype)


def modulate(h, shift, scale):
    return h * (1 + scale[:, None, :]) + shift[:, None, :]


def adaln(cond, w, bias):
    m = jax.nn.silu(cond) @ w + bias
    return jnp.split(m, 6, axis=-1)


def axial_rope_tables(n_tok, dim, dtype):
    t = jnp.arange(n_tok)
    row = (t // GRID_W).astype(jnp.float32)
    col = (t % GRID_W).astype(jnp.float32)
    quarter = dim // 4
    inv = ROPE_BASE ** (-jnp.arange(quarter, dtype=jnp.float32) / quarter)
    ang = jnp.concatenate([row[:, None] * inv, col[:, None] * inv], axis=-1)
    return jnp.cos(ang).astype(dtype), jnp.sin(ang).astype(dtype)


def apply_rope(x, cos, sin):
    half = x.shape[-1] // 2
    shp = (1, cos.shape[0]) + (1,) * (x.ndim - 3) + (half,)
    c, s = cos.reshape(shp), sin.reshape(shp)
    x1, x2 = x[..., :half], x[..., half:]
    return jnp.concatenate([x1 * c - x2 * s, x1 * s + x2 * c], axis=-1)


def map_query_blocks(fn, q, block):
    b, n = q.shape[:2]
    nb = n // block
    qb = jnp.moveaxis(q.reshape((b, nb, block) + q.shape[2:]), 1, 0)
    ob = lax.map(fn, (qb, jnp.arange(nb)))
    return jnp.moveaxis(ob, 0, 1).reshape((b, n) + ob.shape[3:])


def diff_lambda_value(lam_params, layer_idx):
    lq1, lk1, lq2, lk2 = lam_params.astype(jnp.float32)
    lam_init = 0.8 - 0.6 * math.exp(-0.3 * layer_idx)
    lam = jnp.exp(jnp.sum(lq1 * lk1)) - jnp.exp(jnp.sum(lq2 * lk2)) + lam_init
    return lam, lam_init


def diff_attention(q, k, v, lam, lam_init, norm_g):
    scale = DH_A ** -0.5

    def block(args):
        qb, _ = args
        s = jnp.einsum('bqhmd,bkhmd->bhmqk', qb, k).astype(jnp.float32) * scale
        p = jax.nn.softmax(s, axis=-1)
        a = (p[:, :, 0] - lam * p[:, :, 1]).astype(v.dtype)
        return jnp.einsum('bhqk,bkhe->bqhe', a, v)

    o = map_query_blocks(block, q, Q_BLOCK)
    o = rmsnorm(o, norm_g) * (1.0 - lam_init)
    return o.reshape(o.shape[0], o.shape[1], H_A * DV_A)


def _linear_combine(e1, e2):
    a1, b1 = e1
    a2, b2 = e2
    return a1 * a2, a2 * b1 + b2


def ssm_mixer(u, lp, s0):
    b, n, _ = u.shape
    uf = u.astype(jnp.float32)
    uc = uf.reshape(b, n, G_B, CG_B).astype(jnp.complex64)
    y = uf * lp['ssm_d'].astype(jnp.float32)
    finals = []
    for d, reverse in ((0, False), (1, True)):
        lam = lax.complex(lp['ssm_lam_re'][d].astype(jnp.float32), lp['ssm_lam_im'][d].astype(jnp.float32))
        dt = jnp.exp(lp['ssm_log_dt'][d].astype(jnp.float32))[:, None]
        lam_bar = jnp.exp(lam * dt)
        bmat = lax.complex(lp['ssm_b_re'][d].astype(jnp.float32), lp['ssm_b_im'][d].astype(jnp.float32))
        b_bar = ((lam_bar - 1.0) / lam)[..., None] * bmat
        cmat = lax.complex(lp['ssm_c_re'][d].astype(jnp.float32), lp['ssm_c_im'][d].astype(jnp.float32))
        bu = jnp.einsum('gpc,bsgc->bsgp', b_bar, uc)
        edge = n - 1 if reverse else 0
        bu = bu.at[:, edge].add(lam_bar * s0[:, d])
        a = jnp.broadcast_to(lam_bar, bu.shape)
        _, s = lax.associative_scan(_linear_combine, (a, bu), reverse=reverse, axis=1)
        y = y + jnp.real(jnp.einsum('gcp,bsgp->bsgc', cmat, s)).reshape(b, n, W_B)
        finals.append(s[:, 0] if reverse else s[:, -1])
    y = jax.nn.gelu(y)
    z = y @ lp['ssm_w_glu'].astype(jnp.float32)
    out = z[..., :W_B] * jax.nn.sigmoid(z[..., W_B:])
    return out.astype(u.dtype), jnp.stack(finals, axis=1)


def sink_softmax(s, sink_ng):
    col = jnp.broadcast_to(sink_ng.astype(jnp.float32)[None, :, :, None, None], s.shape[:-1] + (1,))
    return jax.nn.softmax(jnp.concatenate([s, col], axis=-1), axis=-1)[..., :-1]


def gqa_sink_attention(q, k, v, sink):
    b = q.shape[0]
    grp = H_C // KVH_C
    scale = DH_C ** -0.5
    sink_ng = sink.reshape(KVH_C, grp)

    def block(args):
        qb, _ = args
        qg = qb.reshape(b, Q_BLOCK, KVH_C, grp, DH_C)
        s = jnp.einsum('bqngd,bknd->bngqk', qg, k).astype(jnp.float32) * scale
        p = sink_softmax(s, sink_ng).astype(v.dtype)
        return jnp.einsum('bngqk,bknd->bqngd', p, v).reshape(b, Q_BLOCK, H_C * DH_C)

    return map_query_blocks(block, q, Q_BLOCK)


def gqa_window_attention(q, k, v, k_ctx, v_ctx, sink):
    b, n = q.shape[:2]
    grp = H_C // KVH_C
    scale = DH_C ** -0.5
    sink_ng = sink.reshape(KVH_C, grp)
    span = 3 * Q_BLOCK
    pad = ((0, 0), (Q_BLOCK, Q_BLOCK), (0, 0), (0, 0))
    k_pad, v_pad = jnp.pad(k, pad), jnp.pad(v, pad)
    qi = jnp.arange(Q_BLOCK)[:, None]
    kj = jnp.arange(span)[None, :]
    band = jnp.abs(kj - Q_BLOCK - qi) <= WINDOW_C

    def block(args):
        qb, i = args
        start = i * Q_BLOCK
        kb = lax.dynamic_slice_in_dim(k_pad, start, span, axis=1)
        vb = lax.dynamic_slice_in_dim(v_pad, start, span, axis=1)
        kpos = start - Q_BLOCK + kj
        mask = band & (kpos >= 0) & (kpos < n)
        qg = qb.reshape(b, Q_BLOCK, KVH_C, grp, DH_C)
        s_loc = jnp.einsum('bqngd,bknd->bngqk', qg, kb).astype(jnp.float32) * scale
        s_loc = jnp.where(mask, s_loc, NEG_INF)
        s_ctx = jnp.einsum('bqngd,bknd->bngqk', qg, k_ctx).astype(jnp.float32) * scale
        p = sink_softmax(jnp.concatenate([s_loc, s_ctx], axis=-1), sink_ng).astype(v.dtype)
        o = (jnp.einsum('bngqk,bknd->bqngd', p[..., :span], vb)
             + jnp.einsum('bngqk,bknd->bqngd', p[..., span:], v_ctx))
        return o.reshape(b, Q_BLOCK, H_C * DH_C)

    return map_query_blocks(block, q, Q_BLOCK)


def plain_attention(q, k, v):
    scale = q.shape[-1] ** -0.5

    def block(args):
        qb, _ = args
        s = jnp.einsum('bqhd,bkhd->bhqk', qb, k).astype(jnp.float32) * scale
        p = jax.nn.softmax(s, axis=-1).astype(v.dtype)
        return jnp.einsum('bhqk,bkhd->bqhd', p, v).reshape(qb.shape[0], qb.shape[1], -1)

    return map_query_blocks(block, q, Q_BLOCK)


def neighborhood_attention(q, k, v, k_ctx, v_ctx, rpb):
    b, n = q.shape[:2]
    rows = n // GRID_W
    wr = min(WIN_R, rows)
    nloc = wr * WIN_C
    scale = DH_D ** -0.5
    kg = k.reshape(b, rows, GRID_W, H_D, DH_D)
    vg = v.reshape(b, rows, GRID_W, H_D, DH_D)
    cols = jnp.arange(GRID_W)
    col_idx = jnp.clip(cols - WIN_C // 2, 0, GRID_W - WIN_C)[:, None] + jnp.arange(WIN_C)[None, :]
    dc = col_idx - cols[:, None]

    def gather(rows_blk):
        nb = jnp.moveaxis(rows_blk[:, :, col_idx], 1, 2)
        return nb.reshape(b, GRID_W, nloc, H_D, DH_D)

    def row_block(args):
        qr, r = args
        r0 = jnp.clip(r - wr // 2, 0, rows - wr)
        k_nb = gather(lax.dynamic_slice_in_dim(kg, r0, wr, axis=1))
        v_nb = gather(lax.dynamic_slice_in_dim(vg, r0, wr, axis=1))
        dr = r0 + jnp.arange(wr) - r
        bias = rpb[:, (dr + WIN_R - 1)[None, :, None], (dc + WIN_C - 1)[:, None, :]]
        bias = bias.reshape(H_D, GRID_W, nloc).astype(jnp.float32)
        s_loc = jnp.einsum('bqhd,bqkhd->bhqk', qr, k_nb).astype(jnp.float32) * scale + bias[None]
        s_ctx = jnp.einsum('bqhd,bkhd->bhqk', qr, k_ctx).astype(jnp.float32) * scale
        p = jax.nn.softmax(jnp.concatenate([s_loc, s_ctx], axis=-1), axis=-1).astype(v.dtype)
        o = (jnp.einsum('bhqk,bqkhd->bqhd', p[..., :nloc], v_nb)
             + jnp.einsum('bhqk,bkhd->bqhd', p[..., nloc:], v_ctx))
        return o.reshape(b, GRID_W, H_D * DH_D)

    return map_query_blocks(row_block, q, GRID_W)


def route(x, router_w, router_b):
    scores = jax.nn.sigmoid((x @ router_w).astype(jnp.float32))
    biased = scores + router_b.astype(jnp.float32)
    grouped = biased.reshape(-1, N_GROUPS, EXPERTS_PER_GROUP)
    group_score = jnp.sum(lax.top_k(grouped, TOP_K)[0], axis=-1)
    g_sel = lax.top_k(group_score, 1)[1]
    in_group = jnp.take_along_axis(grouped, g_sel[:, :, None], axis=1)[:, 0]
    local = lax.top_k(in_group, TOP_K)[1]
    idx = g_sel * EXPERTS_PER_GROUP + local
    w = jnp.take_along_axis(scores, idx, axis=-1)
    return idx, w / jnp.sum(w, axis=-1, keepdims=True)


def moe_ffn(h, lp):
    b, n, d = h.shape
    x = h.reshape(b * n, d)
    idx, wts = route(x, lp['router_w'], lp['router_b'])
    n_assign = b * n * TOP_K
    flat_e = idx.reshape(-1)
    order = jnp.argsort(flat_e)
    e_sorted = flat_e[order]
    tok_sorted = order // TOP_K
    w_sorted = wts.reshape(-1)[order].astype(h.dtype)
    counts = jnp.bincount(flat_e, length=N_EXPERTS)
    padded = (counts + MOE_BLOCK - 1) // MOE_BLOCK * MOE_BLOCK
    ends_p = jnp.cumsum(padded)
    starts_p = ends_p - padded
    starts = jnp.cumsum(counts) - counts
    dest = starts_p[e_sorted] + jnp.arange(n_assign) - starts[e_sorted]
    n_blocks = -(-n_assign // MOE_BLOCK) + N_EXPERTS
    buf = jnp.zeros((n_blocks * MOE_BLOCK, d), h.dtype).at[dest].set(x[tok_sorted])
    block_e = jnp.minimum(jnp.searchsorted(ends_p, jnp.arange(n_blocks) * MOE_BLOCK, side='right'), N_EXPERTS - 1)

    def expert_block(args):
        xb, e = args
        hid = jax.nn.silu(xb @ lp['w_e1'][e]) * (xb @ lp['w_e3'][e])
        return hid @ lp['w_e2'][e]

    yb = lax.map(expert_block, (buf.reshape(n_blocks, MOE_BLOCK, d), block_e))
    y = yb.reshape(n_blocks * MOE_BLOCK, d)[dest] * w_sorted[:, None]
    return jnp.zeros_like(x).at[tok_sorted].add(y).reshape(b, n, d)


def trunk_layer(x, mod, lp, layer_idx, ctx):
    b, n, _ = x.shape
    shift1, scale1, gate1, shift2, scale2, gate2 = mod
    h = modulate(rmsnorm(x, lp['norm1_g']), shift1, scale1)
    offsets = np.cumsum(IN_SIZES)[:-1].tolist()
    (a_q, a_k, a_v, b_u, c_q, c_k, c_v, d_q, d_k, d_v, gate_pre) = jnp.split(h @ lp['w_in'], offsets, axis=-1)
    a_q = a_q.reshape(b, n, H_A, 2, DH_A)
    a_k = a_k.reshape(b, n, H_A, 2, DH_A)
    a_v = a_v.reshape(b, n, H_A, DV_A)
    c_q = c_q.reshape(b, n, H_C, DH_C)
    c_k = c_k.reshape(b, n, KVH_C, DH_C)
    c_v = c_v.reshape(b, n, KVH_C, DH_C)
    d_q = d_q.reshape(b, n, H_D, DH_D)
    d_k = d_k.reshape(b, n, H_D, DH_D)
    d_v = d_v.reshape(b, n, H_D, DH_D)
    lam, lam_init = diff_lambda_value(lp['diff_lambda'], layer_idx)
    if ctx is None:
        s0 = jnp.zeros((b, 2, G_B, P_B), jnp.complex64)
        o_a = diff_attention(a_q, a_k, a_v, lam, lam_init, lp['diff_norm_g'])
        o_b, s_fin = ssm_mixer(b_u, lp, s0)
        o_c = gqa_sink_attention(c_q, c_k, c_v, lp['sink_c'])
        o_d = plain_attention(d_q, d_k, d_v)
        new_ctx = (a_k, a_v, jnp.real(s_fin), jnp.imag(s_fin), c_k, c_v, d_k, d_v)
    else:
        ak_ctx, av_ctx, s_re, s_im, ck_ctx, cv_ctx, dk_ctx, dv_ctx = ctx
        cos_a, sin_a = axial_rope_tables(n, DH_A, x.dtype)
        cos_c, sin_c = axial_rope_tables(n, DH_C, x.dtype)
        k_all = jnp.concatenate([apply_rope(a_k, cos_a, sin_a), ak_ctx.astype(x.dtype)], axis=1)
        v_all = jnp.concatenate([a_v, av_ctx.astype(x.dtype)], axis=1)
        o_a = diff_attention(apply_rope(a_q, cos_a, sin_a), k_all, v_all, lam, lam_init, lp['diff_norm_g'])
        s0 = lax.complex(s_re.astype(jnp.float32), s_im.astype(jnp.float32))
        o_b, _ = ssm_mixer(b_u, lp, s0)
        o_c = gqa_window_attention(apply_rope(c_q, cos_c, sin_c), apply_rope(c_k, cos_c, sin_c), c_v,
                                   ck_ctx.astype(x.dtype), cv_ctx.astype(x.dtype), lp['sink_c'])
        o_d = neighborhood_attention(d_q, d_k, d_v, dk_ctx.astype(x.dtype), dv_ctx.astype(x.dtype), lp['rpb_d'])
        new_ctx = None
    gates = jax.nn.sigmoid(gate_pre.astype(jnp.float32)).astype(x.dtype).reshape(b, n, N_BRANCH, D_MODEL)
    merged = jnp.zeros_like(x)
    for i, o in enumerate((o_a, o_b, o_c, o_d)):
        merged = merged + gates[:, :, i] * (o @ lp['w_branch'][i])
    x = x + gate1[:, None, :] * (merged @ lp['w_out'])
    h2 = modulate(rmsnorm(x, lp['norm2_g']), shift2, scale2)
    x = x + gate2[:, None, :] * moe_ffn(h2, lp)
    return x, new_ctx


def setup_inputs(seed: int = 0) -> dict:
    key = jax.random.key(seed)
    keys = iter(jax.random.split(key, 48))

    def nrm(shape, s=1.0):
        return jax.random.normal(next(keys), shape, jnp.float32) * s

    x_prompt = nrm((BATCH, SEQ, D_MODEL))
    x_sample = nrm((DEC_BATCH, DEC_SEQ, D_MODEL))
    cache_a_k = nrm((DEC_BATCH, DEPTH, PAST_LEN, H_A, 2, DH_A))
    cache_a_v = nrm((DEC_BATCH, DEPTH, PAST_LEN, H_A, DV_A))
    state_ssm_re = nrm((DEC_BATCH, DEPTH, 2, G_B, P_B), 0.5)
    state_ssm_im = nrm((DEC_BATCH, DEPTH, 2, G_B, P_B), 0.5)
    cache_c_k = nrm((DEC_BATCH, DEPTH, PAST_LEN, KVH_C, DH_C))
    cache_c_v = nrm((DEC_BATCH, DEPTH, PAST_LEN, KVH_C, DH_C))
    cache_d_k = nrm((DEC_BATCH, DEPTH, PAST_LEN, H_D, DH_D))
    cache_d_v = nrm((DEC_BATCH, DEPTH, PAST_LEN, H_D, DH_D))
    c = nrm((DEC_BATCH, D_MODEL))
    c_ctx = nrm((D_MODEL,))
    w_mod = nrm((DEPTH, D_MODEL, 6 * D_MODEL), 0.5 * D_MODEL ** -0.5)
    b_mod = nrm((DEPTH, 6 * D_MODEL), 0.01)
    norm1_g = 1.0 + nrm((DEPTH, D_MODEL), 0.02)
    norm2_g = 1.0 + nrm((DEPTH, D_MODEL), 0.02)
    w_in = nrm((DEPTH, D_MODEL, N_IN), D_MODEL ** -0.5)
    diff_lambda = nrm((DEPTH, 4, DH_A), 0.1)
    diff_norm_g = 1.0 + nrm((DEPTH, DV_A), 0.02)
    ssm_lam_re = -0.5 + nrm((DEPTH, 2, G_B, P_B), 0.01)
    ssm_lam_im = jnp.pi * jnp.arange(P_B, dtype=jnp.float32) + nrm((DEPTH, 2, G_B, P_B), 0.01)
    ssm_log_dt = jax.random.uniform(next(keys), (DEPTH, 2, G_B), jnp.float32, math.log(1e-3), math.log(1e-1))
    ssm_b_re = nrm((DEPTH, 2, G_B, P_B, CG_B), (2 * CG_B) ** -0.5)
    ssm_b_im = nrm((DEPTH, 2, G_B, P_B, CG_B), (2 * CG_B) ** -0.5)
    ssm_c_re = nrm((DEPTH, 2, G_B, CG_B, P_B), (2 * P_B) ** -0.5)
    ssm_c_im = nrm((DEPTH, 2, G_B, CG_B, P_B), (2 * P_B) ** -0.5)
    ssm_d = nrm((DEPTH, W_B), 0.5)
    ssm_w_glu = nrm((DEPTH, W_B, 2 * W_B), W_B ** -0.5)
    sink_c = nrm((DEPTH, H_C))
    rpb_d = nrm((DEPTH, H_D, 2 * WIN_R - 1, 2 * WIN_C - 1), 0.1)
    w_branch = nrm((DEPTH, N_BRANCH, BRANCH_W, D_MODEL), BRANCH_W ** -0.5)
    w_out = nrm((DEPTH, D_MODEL, D_MODEL), D_MODEL ** -0.5)
    router_w = nrm((D_MODEL, N_EXPERTS), D_MODEL ** -0.5)
    router_b = nrm((N_EXPERTS,), 0.01)
    w_e1 = nrm((DEPTH, N_EXPERTS, D_MODEL, D_EXPERT), D_MODEL ** -0.5)
    w_e3 = nrm((DEPTH, N_EXPERTS, D_MODEL, D_EXPERT), D_MODEL ** -0.5)
    w_e2 = nrm((DEPTH, N_EXPERTS, D_EXPERT, D_MODEL), D_EXPERT ** -0.5)
    final_g = 1.0 + nrm((D_MODEL,), 0.02)
    return {'x_prompt': x_prompt, 'x_sample': x_sample,
            'cache_a_k': cache_a_k, 'cache_a_v': cache_a_v,
            'state_ssm_re': state_ssm_re, 'state_ssm_im': state_ssm_im,
            'cache_c_k': cache_c_k, 'cache_c_v': cache_c_v,
            'cache_d_k': cache_d_k, 'cache_d_v': cache_d_v,
            'c': c, 'c_ctx': c_ctx, 'w_mod': w_mod, 'b_mod': b_mod,
            'norm1_g': norm1_g, 'norm2_g': norm2_g, 'w_in': w_in,
            'diff_lambda': diff_lambda, 'diff_norm_g': diff_norm_g,
            'ssm_lam_re': ssm_lam_re, 'ssm_lam_im': ssm_lam_im, 'ssm_log_dt': ssm_log_dt,
            'ssm_b_re': ssm_b_re, 'ssm_b_im': ssm_b_im, 'ssm_c_re': ssm_c_re, 'ssm_c_im': ssm_c_im,
            'ssm_d': ssm_d, 'ssm_w_glu': ssm_w_glu, 'sink_c': sink_c, 'rpb_d': rpb_d,
            'w_branch': w_branch, 'w_out': w_out, 'router_w': router_w, 'router_b': router_b,
            'w_e1': w_e1, 'w_e3': w_e3, 'w_e2': w_e2, 'final_g': final_g}


def reference(x_prompt, x_sample, cache_a_k, cache_a_v, state_ssm_re, state_ssm_im, cache_c_k, cache_c_v,
              cache_d_k, cache_d_v, c, c_ctx, w_mod, b_mod, norm1_g, norm2_g, w_in, diff_lambda, diff_norm_g,
              ssm_lam_re, ssm_lam_im, ssm_log_dt, ssm_b_re, ssm_b_im, ssm_c_re, ssm_c_im, ssm_d, ssm_w_glu,
              sink_c, rpb_d, w_branch, w_out, router_w, router_b, w_e1, w_e3, w_e2, final_g):
    y_p, y_s = x_prompt, x_sample
    collected = [[] for _ in range(8)]
    for l in range(DEPTH):
        lp = {'norm1_g': norm1_g[l], 'norm2_g': norm2_g[l], 'w_in': w_in[l],
              'diff_lambda': diff_lambda[l], 'diff_norm_g': diff_norm_g[l],
              'ssm_lam_re': ssm_lam_re[l], 'ssm_lam_im': ssm_lam_im[l], 'ssm_log_dt': ssm_log_dt[l],
              'ssm_b_re': ssm_b_re[l], 'ssm_b_im': ssm_b_im[l], 'ssm_c_re': ssm_c_re[l], 'ssm_c_im': ssm_c_im[l],
              'ssm_d': ssm_d[l], 'ssm_w_glu': ssm_w_glu[l], 'sink_c': sink_c[l], 'rpb_d': rpb_d[l],
              'w_branch': w_branch[l], 'w_out': w_out[l],
              'router_w': router_w, 'router_b': router_b,
              'w_e1': w_e1[l], 'w_e3': w_e3[l], 'w_e2': w_e2[l]}
        mod_ctx = adaln(c_ctx[None, :], w_mod[l], b_mod[l])
        mod_lat = adaln(c, w_mod[l], b_mod[l])
        y_p, ctx_new = trunk_layer(y_p, mod_ctx, lp, l, None)
        cached = (cache_a_k[:, l], cache_a_v[:, l], state_ssm_re[:, l], state_ssm_im[:, l],
                  cache_c_k[:, l], cache_c_v[:, l], cache_d_k[:, l], cache_d_v[:, l])
        y_s, _ = trunk_layer(y_s, mod_lat, lp, l, cached)
        for lst, t in zip(collected, ctx_new):
            lst.append(t)
    new_a_k, new_a_v, new_ssm_re, new_ssm_im, new_c_k, new_c_v, new_d_k, new_d_v = [
        jnp.stack(lst, axis=1) for lst in collected]
    y_prompt = rmsnorm(y_p, final_g)
    y_sample = rmsnorm(y_s, final_g)
    return (y_prompt, y_sample, new_a_k, new_a_v, new_ssm_re, new_ssm_im, new_c_k, new_c_v, new_d_k, new_d_v)
```

```python
import functools
import math

import jax
import jax.numpy as jnp
import numpy as np
from jax import lax
from jax.experimental import pallas as pl
from jax.experimental.pallas import tpu as pltpu

F32 = jnp.float32
BF16 = jnp.bfloat16
HIGHEST = lax.Precision.HIGHEST

D_MODEL = 2048
DEPTH = 2
GRID_W = 64
EPS = 1e-6
ROPE_BASE = 10000.0
NEG_INF = -1e30
H_A, DH_A = 4, 64
DV_A = 2 * DH_A
W_B, CG_B, P_B = 512, 16, 64
G_B = W_B // CG_B
H_C, KVH_C, DH_C, WINDOW_C = 8, 2, 64, 128
H_D, DH_D, WIN_R, WIN_C = 8, 64, 8, 16
BRANCH_W, N_BRANCH = 512, 4
N_EXPERTS, N_GROUPS, TOP_K, D_EXPERT = 32, 8, 2, 512
EXPERTS_PER_GROUP = N_EXPERTS // N_GROUPS
N_QKV = 4352
OFF_A, OFF_U, OFF_C, OFF_D = 0, 1536, 2048, 2816
SSM_T = 16
MOE_ROWS = 256
VMEM_LIMIT = 56 * 1024 * 1024


def _cparams(sem):
    return pltpu.CompilerParams(dimension_semantics=sem, vmem_limit_bytes=VMEM_LIMIT)


def _dot(a, b):
    return jnp.dot(a, b, preferred_element_type=F32)


def _dot_nt(a, b):
    return lax.dot_general(a, b, (((1,), (1,)), ((), ())), preferred_element_type=F32)


def _adaln_kernel(c_ref, w_ref, b_ref, o_ref):
    a = c_ref[...]
    a = a * jax.nn.sigmoid(a)
    o_ref[...] = jnp.dot(a, w_ref[...], precision=HIGHEST, preferred_element_type=F32) + b_ref[...]


def adaln_all(cond, w_mod, b_mod, tn=1024):
    nl, d, n6 = w_mod.shape
    return pl.pallas_call(
        _adaln_kernel,
        grid=(nl, n6 // tn),
        in_specs=[pl.BlockSpec((cond.shape[0], d), lambda l, j: (0, 0)),
                  pl.BlockSpec((None, d, tn), lambda l, j: (l, 0, j)),
                  pl.BlockSpec((None, 1, tn), lambda l, j: (l, 0, j))],
        out_specs=pl.BlockSpec((None, cond.shape[0], tn), lambda l, j: (l, 0, j)),
        out_shape=jax.ShapeDtypeStruct((nl, cond.shape[0], n6), F32),
        compiler_params=_cparams(("arbitrary", "arbitrary")),
        name="adaln",
    )(cond, w_mod, b_mod)


def _norm_mod_kernel(x_ref, g_ref, sh_ref, sc_ref, o_ref):
    x = x_ref[...]
    r = lax.rsqrt(jnp.mean(x * x, axis=-1, keepdims=True) + EPS)
    o_ref[...] = ((x * r * g_ref[...]) * (1.0 + sc_ref[...]) + sh_ref[...]).astype(o_ref.dtype)


def norm_mod(x, g, shift, scale, rows_per_mod, tm=512):
    m, d = x.shape
    tm = min(tm, rows_per_mod)
    per = rows_per_mod // tm
    mod_spec = pl.BlockSpec((None, 1, d), lambda i: (i // per, 0, 0))
    return pl.pallas_call(
        _norm_mod_kernel,
        grid=(m // tm,),
        in_specs=[pl.BlockSpec((tm, d), lambda i: (i, 0)), pl.BlockSpec((1, d), lambda i: (0, 0)), mod_spec, mod_spec],
        out_specs=pl.BlockSpec((tm, d), lambda i: (i, 0)),
        out_shape=jax.ShapeDtypeStruct((m, d), BF16),
        compiler_params=_cparams(("parallel",)),
        name="norm_mod",
    )(x, g, shift, scale)


def _matmul_kernel(a_ref, w_ref, o_ref, *, act):
    acc = _dot(a_ref[...], w_ref[...])
    if act == "sigmoid":
        acc = jax.nn.sigmoid(acc)
    o_ref[...] = acc.astype(o_ref.dtype)


def matmul(a, w, out_dtype, act=None, tm=1024, tn=256, name="matmul"):
    m, k = a.shape
    n = w.shape[1]
    tm = min(tm, m)
    return pl.pallas_call(
        functools.partial(_matmul_kernel, act=act),
        grid=(m // tm, n // tn),
        in_specs=[pl.BlockSpec((tm, k), lambda i, j: (i, 0)), pl.BlockSpec((k, tn), lambda i, j: (0, j))],
        out_specs=pl.BlockSpec((tm, tn), lambda i, j: (i, j)),
        out_shape=jax.ShapeDtypeStruct((m, n), out_dtype),
        compiler_params=_cparams(("parallel", "arbitrary")),
        name=name,
    )(a, w)


def _softmax_parts(scores, extra=None):
    mx = scores[0].max(axis=-1, keepdims=True)
    for s in scores[1:]:
        mx = jnp.maximum(mx, s.max(axis=-1, keepdims=True))
    if extra is not None:
        mx = jnp.maximum(mx, extra)
    ps = [jnp.exp(s - mx) for s in scores]
    den = ps[0].sum(axis=-1, keepdims=True)
    for p in ps[1:]:
        den = den + p.sum(axis=-1, keepdims=True)
    if extra is not None:
        den = den + jnp.exp(extra - mx)
    return ps, den


def _head_rmsnorm(o, g, mult):
    r = lax.rsqrt(jnp.mean(o * o, axis=-1, keepdims=True) + EPS)
    return o * r * g * mult


def _ctx_mixers_kernel(s_ref, lam_ref, ng_ref, sink_ref, oa_ref, oc_ref, od_ref, *, lam_init):
    lam = lam_ref[...]
    for h in range(H_A):
        probs = []
        for mi in range(2):
            c0 = OFF_A + (h * 2 + mi) * DH_A
            q = (s_ref[:, c0:c0 + DH_A] * (DH_A ** -0.5)).astype(BF16)
            k = s_ref[:, 512 + c0:512 + c0 + DH_A].astype(BF16)
            (p,), den = _softmax_parts([_dot_nt(q, k)])
            probs.append(p / den)
        a = (probs[0] - lam * probs[1]).astype(BF16)
        v = s_ref[:, 1024 + h * DV_A:1024 + (h + 1) * DV_A].astype(BF16)
        o = _head_rmsnorm(_dot(a, v), ng_ref[...], 1.0 - lam_init)
        oa_ref[:, h * DV_A:(h + 1) * DV_A] = o.astype(oa_ref.dtype)
    grp = H_C // KVH_C
    for h in range(H_C):
        n = h // grp
        q = (s_ref[:, OFF_C + h * DH_C:OFF_C + (h + 1) * DH_C] * (DH_C ** -0.5)).astype(BF16)
        k = s_ref[:, OFF_C + 512 + n * DH_C:OFF_C + 512 + (n + 1) * DH_C].astype(BF16)
        v = s_ref[:, OFF_C + 640 + n * DH_C:OFF_C + 640 + (n + 1) * DH_C].astype(BF16)
        sink = jnp.broadcast_to(sink_ref[:, h:h + 1], (q.shape[0], 1))
        (p,), den = _softmax_parts([_dot_nt(q, k)], extra=sink)
        o = _dot((p / den).astype(BF16), v)
        oc_ref[:, h * DH_C:(h + 1) * DH_C] = o.astype(oc_ref.dtype)
    for h in range(H_D):
        q = (s_ref[:, OFF_D + h * DH_D:OFF_D + (h + 1) * DH_D] * (DH_D ** -0.5)).astype(BF16)
        k = s_ref[:, OFF_D + 512 + h * DH_D:OFF_D + 512 + (h + 1) * DH_D].astype(BF16)
        v = s_ref[:, OFF_D + 1024 + h * DH_D:OFF_D + 1024 + (h + 1) * DH_D].astype(BF16)
        (p,), den = _softmax_parts([_dot_nt(q, k)])
        o = _dot((p / den).astype(BF16), v)
        od_ref[:, h * DH_D:(h + 1) * DH_D] = o.astype(od_ref.dtype)


def ctx_mixers(slab, lam, lam_init, norm_g, sink):
    b, s, _ = slab.shape
    o_spec = pl.BlockSpec((None, s, BRANCH_W), lambda i: (i, 0, 0))
    o_shape = jax.ShapeDtypeStruct((b, s, BRANCH_W), BF16)
    return pl.pallas_call(
        functools.partial(_ctx_mixers_kernel, lam_init=lam_init),
        grid=(b,),
        in_specs=[pl.BlockSpec((None, s, N_QKV), lambda i: (i, 0, 0)),
                  pl.BlockSpec((1, 1), lambda i: (0, 0)),
                  pl.BlockSpec((1, DV_A), lambda i: (0, 0)),
                  pl.BlockSpec((1, H_C), lambda i: (0, 0))],
        out_specs=[o_spec, o_spec, o_spec],
        out_shape=[o_shape, o_shape, o_shape],
        compiler_params=_cparams(("parallel",)),
        name="ctx_mixers",
    )(slab, lam, norm_g, sink)


def _rope128(x, cos, sin):
    lane = lax.broadcasted_iota(jnp.int32, x.shape, 1)
    other = jnp.where((lane % 64) < 32, pltpu.roll(x, 96, axis=1), pltpu.roll(x, 32, axis=1))
    return x * cos + other * sin


def _lat_prep_kernel(s_ref, cos_ref, sin_ref, a_ref, c_ref, d_ref):
    cos, sin = cos_ref[...], sin_ref[...]

    def put(dst, dst0, src0, rope, scale):
        x = s_ref[:, src0:src0 + 128]
        if rope:
            x = _rope128(x, cos, sin)
        if scale != 1.0:
            x = x * scale
        dst[:, dst0:dst0 + 128] = x.astype(dst.dtype)

    for j in range(12):
        put(a_ref, j * 128, OFF_A + j * 128, j < 8, DH_A ** -0.5 if j < 4 else 1.0)
    for j in range(6):
        if j < 4:
            put(c_ref, j * 128, OFF_C + j * 128, True, DH_C ** -0.5)
        elif j == 4:
            put(c_ref, j * 128, OFF_C + j * 128, True, 1.0)
        else:
            put(c_ref, j * 128, OFF_C + j * 128, False, 1.0)
    for j in range(12):
        put(d_ref, j * 128, OFF_D + j * 128, False, DH_D ** -0.5 if j < 4 else 1.0)


def lat_prep(slab, cos128, sin128, tm=512):
    b, n, _ = slab.shape
    tm = min(tm, n)
    widths = (1536, 768, 1536)
    return pl.pallas_call(
        _lat_prep_kernel,
        grid=(b, n // tm),
        in_specs=[pl.BlockSpec((None, tm, N_QKV), lambda i, j: (i, j, 0)),
                  pl.BlockSpec((tm, 128), lambda i, j: (j, 0)),
                  pl.BlockSpec((tm, 128), lambda i, j: (j, 0))],
        out_specs=[pl.BlockSpec((None, tm, w), lambda i, j: (i, j, 0)) for w in widths],
        out_shape=[jax.ShapeDtypeStruct((b, n, w), BF16) for w in widths],
        compiler_params=_cparams(("parallel", "parallel")),
        name="lat_prep",
    )(slab, cos128, sin128)


def _lat_a_kernel(q_ref, k_ref, v_ref, kc_ref, vc_ref, lam_ref, ng_ref, o_ref, *, tk, lam_init):
    n = k_ref.shape[0]
    tq = q_ref.shape[0]
    q = q_ref[...]
    qs = (q[:, :DH_A], q[:, DH_A:])

    def update(state, kblk, vblk):
        new = []
        for mi in range(2):
            m_old, l_old, acc = state[mi]
            s = _dot_nt(qs[mi], kblk[:, mi * DH_A:(mi + 1) * DH_A])
            m_new = jnp.maximum(m_old, s.max(axis=-1, keepdims=True))
            alpha = jnp.exp(m_old - m_new)
            p = jnp.exp(s - m_new)
            l_new = alpha * l_old + p.sum(axis=-1, keepdims=True)
            acc = alpha * acc + _dot(p.astype(BF16), vblk)
            new.append((m_new, l_new, acc))
        return tuple(new)

    init = tuple((jnp.full((tq, 1), -jnp.inf, F32), jnp.zeros((tq, 1), F32), jnp.zeros((tq, DV_A), F32))
                 for _ in range(2))
    state = update(init, kc_ref[...].astype(BF16), vc_ref[...].astype(BF16))

    def body(j, st):
        r0 = pl.multiple_of(j * tk, tk)
        return update(st, k_ref[pl.ds(r0, tk), :], v_ref[pl.ds(r0, tk), :])

    state = lax.fori_loop(0, n // tk, body, state)
    (_, l1, a1), (_, l2, a2) = state
    o = a1 / l1 - lam_ref[...] * (a2 / l2)
    o_ref[...] = _head_rmsnorm(o, ng_ref[...], 1.0 - lam_init).astype(o_ref.dtype)


def lat_attn_a(a_qkv, ctx_k, ctx_v, layer, lam, lam_init, norm_g, tq=256, tk=512):
    b, n, _ = a_qkv.shape
    past = ctx_k.shape[2]
    tq, tk = min(tq, n), min(tk, n)
    return pl.pallas_call(
        functools.partial(_lat_a_kernel, tk=tk, lam_init=lam_init),
        grid=(b, H_A, n // tq),
        in_specs=[pl.BlockSpec((None, tq, 128), lambda i, h, j: (i, j, h)),
                  pl.BlockSpec((None, n, 128), lambda i, h, j: (i, 0, H_A + h)),
                  pl.BlockSpec((None, n, 128), lambda i, h, j: (i, 0, 2 * H_A + h)),
                  pl.BlockSpec((None, None, past, 128), lambda i, h, j: (i, layer, 0, h)),
                  pl.BlockSpec((None, None, past, 128), lambda i, h, j: (i, layer, 0, h)),
                  pl.BlockSpec((1, 1), lambda i, h, j: (0, 0)),
                  pl.BlockSpec((1, DV_A), lambda i, h, j: (0, 0))],
        out_specs=pl.BlockSpec((None, tq, DV_A), lambda i, h, j: (i, j, h)),
        out_shape=jax.ShapeDtypeStruct((b, n, BRANCH_W), BF16),
        compiler_params=_cparams(("parallel", "parallel", "arbitrary")),
        name="lat_attn_a",
    )(a_qkv, a_qkv, a_qkv, ctx_k, ctx_v, lam, norm_g)


def _lat_c_kernel(q_ref, kv_ref, kc_ref, vc_ref, sink_ref, o_ref):
    blk = q_ref.shape[0]
    nb = kv_ref.shape[0] // blk
    i = pl.program_id(1)
    row = lax.broadcasted_iota(jnp.int32, (blk, blk), 0)
    col = lax.broadcasted_iota(jnp.int32, (blk, blk), 1)
    masks = (jnp.logical_and(col >= row, i > 0), None, jnp.logical_and(col <= row, i < nb - 1))
    pieces = []
    for p, d in enumerate((-1, 0, 1)):
        r0 = pl.multiple_of(jnp.clip(i + d, 0, nb - 1) * blk, blk)
        pieces.append(kv_ref[pl.ds(r0, blk), :])
    kc = kc_ref[...].astype(BF16)
    vc = vc_ref[...].astype(BF16)
    grp = H_C // KVH_C
    for h in range(H_C):
        n = h // grp
        q = q_ref[:, h * DH_C:(h + 1) * DH_C]
        scores = []
        for p in range(3):
            s = _dot_nt(q, pieces[p][:, n * DH_C:(n + 1) * DH_C])
            if masks[p] is not None:
                s = jnp.where(masks[p], s, NEG_INF)
            scores.append(s)
        scores.append(_dot_nt(q, kc[:, n * DH_C:(n + 1) * DH_C]))
        sink = jnp.broadcast_to(sink_ref[:, h:h + 1], (blk, 1))
        ps, den = _softmax_parts(scores, extra=sink)
        o = _dot((ps[3] / den).astype(BF16), vc[:, n * DH_C:(n + 1) * DH_C])
        for p in range(3):
            v = pieces[p][:, KVH_C * DH_C + n * DH_C:KVH_C * DH_C + (n + 1) * DH_C]
            o = o + _dot((ps[p] / den).astype(BF16), v)
        o_ref[:, h * DH_C:(h + 1) * DH_C] = o.astype(o_ref.dtype)


def lat_attn_c(c_qkv, ctx_k, ctx_v, layer, sink):
    b, n, _ = c_qkv.shape
    past = ctx_k.shape[2]
    blk = WINDOW_C
    return pl.pallas_call(
        _lat_c_kernel,
        grid=(b, n // blk),
        in_specs=[pl.BlockSpec((None, blk, 512), lambda i, j: (i, j, 0)),
                  pl.BlockSpec((None, n, 256), lambda i, j: (i, 0, 2)),
                  pl.BlockSpec((None, None, past, 128), lambda i, j: (i, layer, 0, 0)),
                  pl.BlockSpec((None, None, past, 128), lambda i, j: (i, layer, 0, 0)),
                  pl.BlockSpec((1, H_C), lambda i, j: (0, 0))],
        out_specs=pl.BlockSpec((None, blk, BRANCH_W), lambda i, j: (i, j, 0)),
        out_shape=jax.ShapeDtypeStruct((b, n, BRANCH_W), BF16),
        compiler_params=_cparams(("parallel", "arbitrary")),
        name="lat_attn_c",
    )(c_qkv, c_qkv, ctx_k, ctx_v, sink)


def _lat_d_kernel(q_ref, k_ref, v_ref, kc_ref, vc_ref, bias_ref, o_ref, *, rows, wr):
    r = pl.program_id(1)
    r0 = jnp.clip(r - wr // 2, 0, rows - wr)
    t0 = pl.multiple_of(r0 * GRID_W, GRID_W)
    kb = k_ref[pl.ds(t0, wr * GRID_W), :]
    vb = v_ref[pl.ds(t0, wr * GRID_W), :]
    kc = kc_ref[...].astype(BF16)
    vc = vc_ref[...].astype(BF16)
    for h in range(H_D):
        sl = slice(h * DH_D, (h + 1) * DH_D)
        q = q_ref[:, sl]
        s_loc = _dot_nt(q, kb[:, sl]) + bias_ref[h]
        s_ctx = _dot_nt(q, kc[:, sl])
        (p_loc, p_ctx), den = _softmax_parts([s_loc, s_ctx])
        o = _dot((p_loc / den).astype(BF16), vb[:, sl]) + _dot((p_ctx / den).astype(BF16), vc[:, sl])
        o_ref[:, sl] = o.astype(o_ref.dtype)


def lat_attn_d(d_qkv, ctx_k, ctx_v, layer, bias_tab):
    b, n, _ = d_qkv.shape
    past = ctx_k.shape[2]
    rows = n // GRID_W
    wr = min(WIN_R, rows)

    def bias_idx(i, r):
        return (jnp.clip(r - wr // 2, 0, rows - wr) - r + wr - 1, 0, 0, 0)

    return pl.pallas_call(
        functools.partial(_lat_d_kernel, rows=rows, wr=wr),
        grid=(b, rows),
        in_specs=[pl.BlockSpec((None, GRID_W, 512), lambda i, r: (i, r, 0)),
                  pl.BlockSpec((None, n, 512), lambda i, r: (i, 0, 1)),
                  pl.BlockSpec((None, n, 512), lambda i, r: (i, 0, 2)),
                  pl.BlockSpec((None, None, past, 512), lambda i, r: (i, layer, 0, 0)),
                  pl.BlockSpec((None, None, past, 512), lambda i, r: (i, layer, 0, 0)),
                  pl.BlockSpec((None, H_D, GRID_W, wr * GRID_W), bias_idx)],
        out_specs=pl.BlockSpec((None, GRID_W, BRANCH_W), lambda i, r: (i, r, 0)),
        out_shape=jax.ShapeDtypeStruct((b, n, BRANCH_W), BF16),
        compiler_params=_cparams(("parallel", "arbitrary")),
        name="lat_attn_d",
    )(d_qkv, d_qkv, d_qkv, ctx_k, ctx_v, bias_tab)


def neighbourhood_bias(rpb, rows):
    wr = min(WIN_R, rows)
    cols = np.arange(GRID_W)
    cstart = np.clip(cols - WIN_C // 2, 0, GRID_W - WIN_C)
    cc = np.arange(GRID_W)
    inside = (cc[None, :] >= cstart[:, None]) & (cc[None, :] < cstart[:, None] + WIN_C)
    dc_idx = np.clip(cc[None, :] - cols[:, None] + WIN_C - 1, 0, 2 * WIN_C - 2)
    off = np.arange(-(wr - 1), 1)
    dr_idx = off[:, None] + np.arange(wr)[None, :] + WIN_R - 1
    tab = rpb.astype(F32)[:, dr_idx][:, :, :, dc_idx]
    tab = jnp.where(inside[None, None, None], tab, NEG_INF)
    tab = jnp.transpose(tab, (1, 0, 3, 2, 4))
    return tab.reshape(wr, H_D, GRID_W, wr * GRID_W)


def ssm_matrices(lam_re, lam_im, log_dt, b_re, b_im, c_re, c_im):
    t = SSM_T
    out = []
    for d in range(2):
        dt = jnp.exp(log_dt[d].astype(F32))[:, None]
        lr, li = lam_re[d].astype(F32), lam_im[d].astype(F32)
        zr, zi = lr * dt, li * dt

        def lam_pow(tau):
            mag = jnp.exp(zr * tau)
            return mag * jnp.cos(zi * tau), mag * jnp.sin(zi * tau)

        lbr, lbi = lam_pow(1.0)
        den = lr * lr + li * li
        nr, ni = lbr - 1.0, lbi
        qr, qi = (nr * lr + ni * li) / den, (ni * lr - nr * li) / den
        br, bi = b_re[d].astype(F32), b_im[d].astype(F32)
        bbr = qr[..., None] * br - qi[..., None] * bi
        bbi = qr[..., None] * bi + qi[..., None] * br
        cr, ci = c_re[d].astype(F32), c_im[d].astype(F32)
        taus = jnp.arange(t + 1, dtype=F32)
        pw = [lam_pow(taus[i]) for i in range(t + 1)]
        ar = jnp.stack([p[0] for p in pw])
        ai = jnp.stack([p[1] for p in pw])
        pr = ar[:t, :, :, None] * bbr[None] - ai[:t, :, :, None] * bbi[None]
        pi = ar[:t, :, :, None] * bbi[None] + ai[:t, :, :, None] * bbr[None]
        kk = (jnp.einsum('gep,tgpc->tgec', cr, pr, precision=HIGHEST)
              - jnp.einsum('gep,tgpc->tgec', ci, pi, precision=HIGHEST))
        jj = np.arange(t)[:, None]
        tt = np.arange(t)[None, :]
        lag = (tt - jj) if d == 0 else (jj - tt)
        valid = lag >= 0
        m = kk[np.clip(lag, 0, t - 1)]
        m = jnp.where(valid[:, :, None, None, None], m, 0.0)
        m = jnp.transpose(m, (2, 0, 4, 1, 3)).reshape(G_B, t * CG_B, t * CG_B)
        epow = (t - 1 - np.arange(t)) if d == 0 else np.arange(t)
        er = jnp.transpose(pr[epow], (1, 0, 3, 2)).reshape(G_B, t * CG_B, P_B)
        ei = jnp.transpose(pi[epow], (1, 0, 3, 2)).reshape(G_B, t * CG_B, P_B)
        fpow = (np.arange(t) + 1) if d == 0 else (t - np.arange(t))
        far, fai = ar[fpow], ai[fpow]
        fr = cr[None] * far[:, :, None, :] - ci[None] * fai[:, :, None, :]
        fi = -(cr[None] * fai[:, :, None, :] + ci[None] * far[:, :, None, :])
        fr = jnp.transpose(fr, (1, 3, 0, 2)).reshape(G_B, P_B, t * CG_B)
        fi = jnp.transpose(fi, (1, 3, 0, 2)).reshape(G_B, P_B, t * CG_B)
        out.append((m, er, ei, fr, fi, ar[t], ai[t]))
    return out


def _ssm_kernel(u_ref, mf_ref, mb_ref, efr_ref, efi_ref, ebr_ref, ebi_ref, ffr_ref, ffi_ref, fbr_ref, fbi_ref,
                dr_ref, di_ref, s0r_ref, s0i_ref, y_ref, sr_out, si_out, xr_s, xi_s, cr_s, ci_s, *, nb):
    nc, b2, p = xr_s.shape
    u = u_ref[...]
    rid = lax.broadcasted_iota(jnp.int32, (nc * b2, 1), 0)
    isf = (rid % b2) < nb
    xr_s[...] = jnp.where(isf, _dot(u, efr_ref[...]), _dot(u, ebr_ref[...])).reshape(nc, b2, p)
    xi_s[...] = jnp.where(isf, _dot(u, efi_ref[...]), _dot(u, ebi_ref[...])).reshape(nc, b2, p)
    dr, di = dr_ref[...], di_ref[...]

    def body(k, carry):
        sr, si = carry
        cr_s[k] = sr
        ci_s[k] = si
        return dr * sr - di * si + xr_s[k], dr * si + di * sr + xi_s[k]

    sr, si = lax.fori_loop(0, nc, body, (s0r_ref[...], s0i_ref[...]))
    sr_out[...] = sr
    si_out[...] = si
    cr = cr_s[...].reshape(nc * b2, p).astype(BF16)
    ci = ci_s[...].reshape(nc * b2, p).astype(BF16)
    yf = _dot(u, mf_ref[...]) + _dot(cr, ffr_ref[...]) + _dot(ci, ffi_ref[...])
    yb = _dot(u, mb_ref[...]) + _dot(cr, fbr_ref[...]) + _dot(ci, fbi_ref[...])
    y_ref[...] = jnp.where(isf, yf, yb)


def ssm_scan(u, mats, s0_re, s0_im):
    nb, s, _ = u.shape
    t = SSM_T
    nc = s // t
    b2 = 2 * nb
    (mf, efr, efi, ffr, ffi, dfr, dfi), (mb, ebr, ebi, fbr, fbi, dbr, dbi) = mats
    ug = u.astype(BF16).reshape(nb, nc, t, G_B, CG_B)
    ug = jnp.transpose(ug, (3, 1, 0, 2, 4)).reshape(G_B, nc, nb, t * CG_B)
    u2 = jnp.concatenate([ug, ug[:, ::-1]], axis=2).reshape(G_B, nc * b2, t * CG_B)

    def per_row(f, bk):
        return jnp.concatenate([jnp.broadcast_to(f[:, None], (G_B, nb, P_B)),
                                jnp.broadcast_to(bk[:, None], (G_B, nb, P_B))], axis=1)

    dr, di = per_row(dfr, dbr), per_row(dfi, dbi)
    s0r = jnp.transpose(s0_re.astype(F32), (2, 1, 0, 3)).reshape(G_B, b2, P_B)
    s0i = jnp.transpose(s0_im.astype(F32), (2, 1, 0, 3)).reshape(G_B, b2, P_B)
    bf = lambda a: a.astype(BF16)
    g3 = lambda a, c: pl.BlockSpec((None, a, c), lambda g: (g, 0, 0))
    tc = t * CG_B
    y2, sr, si = pl.pallas_call(
        functools.partial(_ssm_kernel, nb=nb),
        grid=(G_B,),
        in_specs=[g3(nc * b2, tc), g3(tc, tc), g3(tc, tc)] + [g3(tc, P_B)] * 4 + [g3(P_B, tc)] * 4
        + [g3(b2, P_B)] * 4,
        out_specs=[g3(nc * b2, tc), g3(b2, P_B), g3(b2, P_B)],
        out_shape=[jax.ShapeDtypeStruct((G_B, nc * b2, tc), F32),
                   jax.ShapeDtypeStruct((G_B, b2, P_B), F32), jax.ShapeDtypeStruct((G_B, b2, P_B), F32)],
        scratch_shapes=[pltpu.VMEM((nc, b2, P_B), F32) for _ in range(4)],
        compiler_params=_cparams(("parallel",)),
        name="ssm_scan",
    )(u2, bf(mf), bf(mb), bf(efr), bf(efi), bf(ebr), bf(ebi), bf(ffr), bf(ffi), bf(fbr), bf(fbi),
      dr, di, s0r, s0i)
    y2 = y2.reshape(G_B, nc, b2, t, CG_B)
    y = y2[:, :, :nb] + y2[:, ::-1, nb:]
    y = jnp.transpose(y, (2, 1, 3, 0, 4)).reshape(nb, s, W_B)
    fin_r = jnp.transpose(sr.reshape(G_B, 2, nb, P_B), (2, 1, 0, 3))
    fin_i = jnp.transpose(si.reshape(G_B, 2, nb, P_B), (2, 1, 0, 3))
    return y, fin_r, fin_i


def _merge_kernel(oa_ref, oc_ref, od_ref, u_ref, ys_ref, gate_ref, dsk_ref, wglu_ref, wb_ref, o_ref):
    y = jax.nn.gelu(u_ref[...] * dsk_ref[...] + ys_ref[...])
    z = _dot(y.astype(BF16), wglu_ref[...])
    ob = (z[:, :W_B] * jax.nn.sigmoid(z[:, W_B:])).astype(BF16)
    acc = None
    for i, o in enumerate((oa_ref[...], ob, oc_ref[...], od_ref[...])):
        term = gate_ref[:, i * D_MODEL:(i + 1) * D_MODEL].astype(F32) * _dot(o, wb_ref[i])
        acc = term if acc is None else acc + term
    o_ref[...] = acc.astype(o_ref.dtype)


def branch_merge(o_a, o_c, o_d, slab, y_ssm, gates, ssm_d, w_glu, w_branch, tm=256):
    m = o_a.shape[0]
    tm = min(tm, m)
    row = lambda w: pl.BlockSpec((tm, w), lambda i: (i, 0))
    return pl.pallas_call(
        _merge_kernel,
        grid=(m // tm,),
        in_specs=[row(BRANCH_W), row(BRANCH_W), row(BRANCH_W),
                  pl.BlockSpec((tm, W_B), lambda i: (i, OFF_U // W_B)),
                  row(W_B), row(N_BRANCH * D_MODEL),
                  pl.BlockSpec((1, W_B), lambda i: (0, 0)),
                  pl.BlockSpec((W_B, 2 * W_B), lambda i: (0, 0)),
                  pl.BlockSpec((N_BRANCH, BRANCH_W, D_MODEL), lambda i: (0, 0, 0))],
        out_specs=row(D_MODEL),
        out_shape=jax.ShapeDtypeStruct((m, D_MODEL), BF16),
        compiler_params=_cparams(("parallel",)),
        name="branch_merge",
    )(o_a, o_c, o_d, slab, y_ssm, gates, ssm_d, w_glu, w_branch)


def _out_kernel(x_ref, mg_ref, w_ref, g1_ref, n2_ref, sh_ref, sc_ref, rw_ref, xo_ref, h_ref, sco_ref):
    x = x_ref[...] + g1_ref[...] * _dot(mg_ref[...], w_ref[...])
    xo_ref[...] = x
    r = lax.rsqrt(jnp.mean(x * x, axis=-1, keepdims=True) + EPS)
    h = (x * r * n2_ref[...]) * (1.0 + sc_ref[...]) + sh_ref[...]
    h_ref[...] = h.astype(h_ref.dtype)
    logits = jnp.dot(h, rw_ref[...], precision=HIGHEST, preferred_element_type=F32)
    sco_ref[...] = jax.nn.sigmoid(logits)


def out_proj(x, merged, w_out, gate1, norm2_g, shift2, scale2, router_w, rows_per_mod, tm=256):
    m, d = x.shape
    tm = min(tm, rows_per_mod)
    per = rows_per_mod // tm
    mod = pl.BlockSpec((None, 1, d), lambda i: (i // per, 0, 0))
    row = lambda w: pl.BlockSpec((tm, w), lambda i: (i, 0))
    ne = router_w.shape[1]
    return pl.pallas_call(
        _out_kernel,
        grid=(m // tm,),
        in_specs=[row(d), row(d), pl.BlockSpec((d, d), lambda i: (0, 0)), mod,
                  pl.BlockSpec((1, d), lambda i: (0, 0)), mod, mod,
                  pl.BlockSpec((d, ne), lambda i: (0, 0))],
        out_specs=[row(d), row(d), row(ne)],
        out_shape=[jax.ShapeDtypeStruct((m, d), F32), jax.ShapeDtypeStruct((m, d), BF16),
                   jax.ShapeDtypeStruct((m, ne), F32)],
        compiler_params=_cparams(("parallel",)),
        name="out_proj",
    )(x, merged, w_out, gate1, norm2_g, shift2, scale2, router_w)


def _experts_kernel(be_ref, nu_ref, x_ref, w1_ref, w3_ref, w2_ref, o_ref, w1_s, w3_s, w2_s):
    i = pl.program_id(0)
    changed = jnp.logical_or(i == 0, be_ref[i] != be_ref[jnp.maximum(i - 1, 0)])

    @pl.when(jnp.logical_and(changed, i < nu_ref[0]))
    def _():
        w1_s[...] = w1_ref[...].astype(BF16)
        w3_s[...] = w3_ref[...].astype(BF16)
        w2_s[...] = w2_ref[...].astype(BF16)

    @pl.when(i < nu_ref[0])
    def _():
        x = x_ref[...]
        a = _dot(x, w1_s[...])
        hid = (a * jax.nn.sigmoid(a)) * _dot(x, w3_s[...])
        o_ref[...] = _dot(hid.astype(BF16), w2_s[...]).astype(o_ref.dtype)

    @pl.when(i >= nu_ref[0])
    def _():
        o_ref[...] = jnp.zeros_like(o_ref)


def experts(buf, block_e, n_used, w1, w3, w2, layer):
    rows, d = buf.shape
    f = w1.shape[-1]
    nblk = rows // MOE_ROWS
    grid_spec = pltpu.PrefetchScalarGridSpec(
        num_scalar_prefetch=2,
        grid=(nblk,),
        in_specs=[pl.BlockSpec((MOE_ROWS, d), lambda i, be, nu: (i, 0)),
                  pl.BlockSpec((None, None, d, f), lambda i, be, nu: (layer, be[i], 0, 0)),
                  pl.BlockSpec((None, None, d, f), lambda i, be, nu: (layer, be[i], 0, 0)),
                  pl.BlockSpec((None, None, f, d), lambda i, be, nu: (layer, be[i], 0, 0))],
        out_specs=pl.BlockSpec((MOE_ROWS, d), lambda i, be, nu: (i, 0)),
        scratch_shapes=[pltpu.VMEM((d, f), BF16), pltpu.VMEM((d, f), BF16), pltpu.VMEM((f, d), BF16)],
    )
    return pl.pallas_call(
        _experts_kernel,
        grid_spec=grid_spec,
        out_shape=jax.ShapeDtypeStruct((rows, d), BF16),
        compiler_params=_cparams(("arbitrary",)),
        name="experts",
    )(block_e, n_used, buf, w1, w3, w2)


def _combine_kernel(x_ref, y0_ref, y1_ref, w_ref, g2_ref, fg_ref, o_ref, *, final):
    w = w_ref[...]
    y = y0_ref[...].astype(F32) * w[:, 0:1] + y1_ref[...].astype(F32) * w[:, 1:2]
    x = x_ref[...] + g2_ref[...] * y
    if final:
        r = lax.rsqrt(jnp.mean(x * x, axis=-1, keepdims=True) + EPS)
        x = x * r * fg_ref[...]
    o_ref[...] = x


def moe_combine(x, y0, y1, wts, gate2, final_g, rows_per_mod, final, tm=512):
    m, d = x.shape
    tm = min(tm, rows_per_mod)
    per = rows_per_mod // tm
    row = lambda w: pl.BlockSpec((tm, w), lambda i: (i, 0))
    return pl.pallas_call(
        functools.partial(_combine_kernel, final=final),
        grid=(m // tm,),
        in_specs=[row(d), row(d), row(d), row(TOP_K),
                  pl.BlockSpec((None, 1, d), lambda i: (i // per, 0, 0)),
                  pl.BlockSpec((1, d), lambda i: (0, 0))],
        out_specs=row(d),
        out_shape=jax.ShapeDtypeStruct((m, d), F32),
        compiler_params=_cparams(("parallel",)),
        name="moe_combine",
    )(x, y0, y1, wts, gate2, final_g)


def route_from_scores(scores, router_b):
    biased = scores + router_b.astype(F32)
    grouped = biased.reshape(-1, N_GROUPS, EXPERTS_PER_GROUP)
    group_score = jnp.sum(lax.top_k(grouped, TOP_K)[0], axis=-1)
    g_sel = lax.top_k(group_score, 1)[1]
    in_group = jnp.take_along_axis(grouped, g_sel[:, :, None], axis=1)[:, 0]
    local = lax.top_k(in_group, TOP_K)[1]
    idx = g_sel * EXPERTS_PER_GROUP + local
    w = jnp.take_along_axis(scores, idx, axis=-1)
    return idx, w / jnp.sum(w, axis=-1, keepdims=True)


def moe_dispatch(idx):
    t = idx.shape[0]
    n_assign = t * TOP_K
    flat_e = idx.reshape(-1).astype(jnp.int32)
    order = jnp.argsort(flat_e)
    e_sorted = flat_e[order]
    counts = jnp.bincount(flat_e, length=N_EXPERTS).astype(jnp.int32)
    padded = (counts + MOE_ROWS - 1) // MOE_ROWS * MOE_ROWS
    ends_p = jnp.cumsum(padded)
    starts_p = ends_p - padded
    starts = jnp.cumsum(counts) - counts
    dest = (starts_p[e_sorted] + jnp.arange(n_assign, dtype=jnp.int32) - starts[e_sorted]).astype(jnp.int32)
    n_blocks = -(-n_assign // MOE_ROWS) + N_EXPERTS
    src_tok = jnp.zeros((n_blocks * MOE_ROWS,), jnp.int32).at[dest].set((order // TOP_K).astype(jnp.int32))
    dest_by_assign = jnp.zeros((n_assign,), jnp.int32).at[order].set(dest).reshape(t, TOP_K)
    block_e = jnp.minimum(jnp.searchsorted(ends_p, jnp.arange(n_blocks, dtype=jnp.int32) * MOE_ROWS, side='right'),
                          N_EXPERTS - 1).astype(jnp.int32)
    n_used = (ends_p[-1:] // MOE_ROWS).astype(jnp.int32)
    return src_tok, dest_by_assign, block_e, n_used


def rope_tables(n_tok):
    t = jnp.arange(n_tok)
    row = (t // GRID_W).astype(F32)
    col = (t % GRID_W).astype(F32)
    quarter = DH_A // 4
    inv = ROPE_BASE ** (-jnp.arange(quarter, dtype=F32) / quarter)
    ang = jnp.concatenate([row[:, None] * inv, col[:, None] * inv], axis=-1)
    cos, sin = jnp.cos(ang), jnp.sin(ang)
    cos128 = jnp.tile(jnp.concatenate([cos, cos], axis=-1), (1, 2))
    sin128 = jnp.tile(jnp.concatenate([-sin, sin], axis=-1), (1, 2))
    return cos128, sin128


def trunk_layer(x, mod, lw, layer, ctx, final):
    b, n, d = x.shape
    m = b * n
    shift1, scale1, gate1, shift2, scale2, gate2 = mod
    rows_per_mod = m if shift1.shape[0] == 1 else n
    xf = x.reshape(m, d)
    h = norm_mod(xf, lw['norm1_g'], shift1, scale1, rows_per_mod)
    slab = matmul(h, lw['w_qkv'], F32, tn=256, name="in_proj_qkv")
    gates = matmul(h, lw['w_gate'], BF16, act="sigmoid", tn=512, name="in_proj_gate")
    slab3 = slab.reshape(b, n, N_QKV)
    u = slab3[:, :, OFF_U:OFF_U + W_B]
    lam, lam_init = lw['lam'], lw['lam_init']
    if ctx is None:
        o_a, o_c, o_d = ctx_mixers(slab3, lam, lam_init, lw['diff_norm_g'], lw['sink_c'])
        s0 = jnp.zeros((b, 2, G_B, P_B), F32)
        y_ssm, fin_r, fin_i = ssm_scan(u, lw['ssm_mats'], s0, s0)
        new_ctx = (slab3[:, :, 512:1024].reshape(b, n, H_A, 2, DH_A),
                   slab3[:, :, 1024:1536].reshape(b, n, H_A, DV_A),
                   fin_r, fin_i,
                   slab3[:, :, OFF_C + 512:OFF_C + 640].reshape(b, n, KVH_C, DH_C),
                   slab3[:, :, OFF_C + 640:OFF_C + 768].reshape(b, n, KVH_C, DH_C),
                   slab3[:, :, OFF_D + 512:OFF_D + 1024].reshape(b, n, H_D, DH_D),
                   slab3[:, :, OFF_D + 1024:OFF_D + 1536].reshape(b, n, H_D, DH_D))
    else:
        ak, av, s_re, s_im, ck, cv, dk, dv = ctx
        a_qkv, c_qkv, d_qkv = lat_prep(slab3, lw['cos128'], lw['sin128'])
        o_a = lat_attn_a(a_qkv, ak, av, layer, lam, lam_init, lw['diff_norm_g'])
        o_c = lat_attn_c(c_qkv, ck, cv, layer, lw['sink_c'])
        o_d = lat_attn_d(d_qkv, dk, dv, layer, lw['bias_d'])
        y_ssm, _, _ = ssm_scan(u, lw['ssm_mats'], s_re[:, layer], s_im[:, layer])
        new_ctx = None
    merged = branch_merge(o_a.reshape(m, -1), o_c.reshape(m, -1), o_d.reshape(m, -1), slab,
                          y_ssm.reshape(m, W_B), gates, lw['ssm_d'], lw['w_glu'], lw['w_branch'])
    x1, h2, scores = out_proj(xf, merged, lw['w_out'], gate1, lw['norm2_g'], shift2, scale2, lw['router_w'],
                              rows_per_mod)
    idx, wts = route_from_scores(scores[:, :N_EXPERTS], lw['router_b'])
    src_tok, dest2, block_e, n_used = moe_dispatch(idx)
    yb = experts(h2[src_tok], block_e, n_used, lw['w_e1'], lw['w_e3'], lw['w_e2'], layer)
    x2 = moe_combine(x1, yb[dest2[:, 0]], yb[dest2[:, 1]], wts, gate2, lw['final_g'], rows_per_mod, final)
    return x2.reshape(b, n, d), new_ctx


def kernel(x_prompt, x_sample, cache_a_k, cache_a_v, state_ssm_re, state_ssm_im, cache_c_k, cache_c_v, cache_d_k, cache_d_v, c, c_ctx, w_mod, b_mod, norm1_g, norm2_g, w_in, diff_lambda, diff_norm_g, ssm_lam_re, ssm_lam_im, ssm_log_dt, ssm_b_re, ssm_b_im, ssm_c_re, ssm_c_im, ssm_d, ssm_w_glu, sink_c, rpb_d, w_branch, w_out, router_w, router_b, w_e1, w_e3, w_e2, final_g):
    depth = w_in.shape[0]
    db, dn, d = x_sample.shape
    past = cache_a_k.shape[2]
    cond = jnp.concatenate([c_ctx[None, :], c], axis=0).astype(F32)
    n_cond = cond.shape[0]
    cond = jnp.pad(cond, ((0, -n_cond % 8), (0, 0)))
    mod_all = adaln_all(cond, w_mod, b_mod[:, None, :])
    cos128, sin128 = rope_tables(dn)
    router_pad = jnp.pad(router_w.astype(F32), ((0, 0), (0, 128 - N_EXPERTS)))
    ctx_caches = (cache_a_k.reshape(db, depth, past, H_A * 2 * DH_A), cache_a_v.reshape(db, depth, past, H_A * DV_A),
                  state_ssm_re, state_ssm_im,
                  cache_c_k.reshape(db, depth, past, KVH_C * DH_C), cache_c_v.reshape(db, depth, past, KVH_C * DH_C),
                  cache_d_k.reshape(db, depth, past, H_D * DH_D), cache_d_v.reshape(db, depth, past, H_D * DH_D))
    y_p, y_s = x_prompt, x_sample
    collected = [[] for _ in range(8)]
    for l in range(depth):
        lq1, lk1, lq2, lk2 = diff_lambda[l].astype(F32)
        lam_init = 0.8 - 0.6 * math.exp(-0.3 * l)
        lam = (jnp.exp(jnp.sum(lq1 * lk1)) - jnp.exp(jnp.sum(lq2 * lk2)) + lam_init).reshape(1, 1)
        w_in_b = w_in[l].astype(BF16)
        lw = {
            'norm1_g': norm1_g[l][None, :], 'norm2_g': norm2_g[l][None, :], 'final_g': final_g[None, :],
            'w_qkv': w_in_b[:, :N_QKV], 'w_gate': w_in_b[:, N_QKV:],
            'lam': lam, 'lam_init': lam_init, 'diff_norm_g': diff_norm_g[l][None, :],
            'sink_c': sink_c[l][None, :].astype(F32),
            'ssm_mats': ssm_matrices(ssm_lam_re[l], ssm_lam_im[l], ssm_log_dt[l], ssm_b_re[l], ssm_b_im[l],
                                     ssm_c_re[l], ssm_c_im[l]),
            'ssm_d': ssm_d[l][None, :].astype(F32), 'w_glu': ssm_w_glu[l].astype(BF16),
            'w_branch': w_branch[l].astype(BF16), 'w_out': w_out[l].astype(BF16),
            'router_w': router_pad, 'router_b': router_b,
            'w_e1': w_e1, 'w_e3': w_e3, 'w_e2': w_e2,
            'cos128': cos128, 'sin128': sin128,
            'bias_d': neighbourhood_bias(rpb_d[l], dn // GRID_W),
        }
        mods = jnp.split(mod_all[l], 6, axis=-1)
        mod_ctx = [mm[0:1][:, None, :] for mm in mods]
        mod_lat = [mm[1:n_cond][:, None, :] for mm in mods]
        final = l == depth - 1
        y_p, ctx_new = trunk_layer(y_p, mod_ctx, lw, l, None, final)
        y_s, _ = trunk_layer(y_s, mod_lat, lw, l, ctx_caches, final)
        for lst, t in zip(collected, ctx_new):
            lst.append(t)
    outs = [jnp.stack(lst, axis=1) for lst in collected]
    return (y_p, y_s, *outs)
```

```python
import functools
import math

import jax
import jax.numpy as jnp
import numpy as np
from jax import lax
from jax.experimental import pallas as pl
from jax.experimental.pallas import tpu as pltpu

F32 = jnp.float32
BF16 = jnp.bfloat16
HIGHEST = lax.Precision.HIGHEST

D_MODEL = 2048
DEPTH = 2
GRID_W = 64
EPS = 1e-6
ROPE_BASE = 10000.0
NEG_INF = -1e30
H_A, DH_A = 4, 64
DV_A = 2 * DH_A
W_B, CG_B, P_B = 512, 16, 64
G_B = W_B // CG_B
H_C, KVH_C, DH_C, WINDOW_C = 8, 2, 64, 128
H_D, DH_D, WIN_R, WIN_C = 8, 64, 8, 16
BRANCH_W, N_BRANCH = 512, 4
N_EXPERTS, N_GROUPS, TOP_K, D_EXPERT = 32, 8, 2, 512
EXPERTS_PER_GROUP = N_EXPERTS // N_GROUPS
N_QKV = 4352
OFF_A, OFF_U, OFF_C, OFF_D = 0, 1536, 2048, 2816
SSM_T = 16
MOE_ROWS = 256
VMEM_LIMIT = 56 * 1024 * 1024
LOG2E = math.log2(math.e)


def _cparams(sem):
    return pltpu.CompilerParams(dimension_semantics=sem, vmem_limit_bytes=VMEM_LIMIT)


def _dot(a, b):
    return jnp.dot(a, b, preferred_element_type=F32)


def _dot_nt(a, b):
    return lax.dot_general(a, b, (((1,), (1,)), ((), ())), preferred_element_type=F32)


def _adaln_kernel(c_ref, w_ref, b_ref, o_ref):
    a = c_ref[...]
    a = a * jax.nn.sigmoid(a)
    o_ref[...] = jnp.dot(a, w_ref[...], precision=HIGHEST, preferred_element_type=F32) + b_ref[...]


def adaln_all(cond, w_mod, b_mod, tn=1024):
    nl, d, n6 = w_mod.shape
    return pl.pallas_call(
        _adaln_kernel,
        grid=(nl, n6 // tn),
        in_specs=[pl.BlockSpec((cond.shape[0], d), lambda l, j: (0, 0)),
                  pl.BlockSpec((None, d, tn), lambda l, j: (l, 0, j)),
                  pl.BlockSpec((None, 1, tn), lambda l, j: (l, 0, j))],
        out_specs=pl.BlockSpec((None, cond.shape[0], tn), lambda l, j: (l, 0, j)),
        out_shape=jax.ShapeDtypeStruct((nl, cond.shape[0], n6), F32),
        compiler_params=_cparams(("arbitrary", "arbitrary")),
        name="adaln",
    )(cond, w_mod, b_mod)


def _norm_mod_kernel(x_ref, g_ref, sh_ref, sc_ref, o_ref):
    x = x_ref[...]
    r = lax.rsqrt(jnp.mean(x * x, axis=-1, keepdims=True) + EPS)
    o_ref[...] = ((x * r * g_ref[...]) * (1.0 + sc_ref[...]) + sh_ref[...]).astype(o_ref.dtype)


def norm_mod(x, g, shift, scale, rows_per_mod, tm=512):
    m, d = x.shape
    tm = min(tm, rows_per_mod)
    per = rows_per_mod // tm
    mod_spec = pl.BlockSpec((None, 1, d), lambda i: (i // per, 0, 0))
    return pl.pallas_call(
        _norm_mod_kernel,
        grid=(m // tm,),
        in_specs=[pl.BlockSpec((tm, d), lambda i: (i, 0)), pl.BlockSpec((1, d), lambda i: (0, 0)), mod_spec, mod_spec],
        out_specs=pl.BlockSpec((tm, d), lambda i: (i, 0)),
        out_shape=jax.ShapeDtypeStruct((m, d), BF16),
        compiler_params=_cparams(("parallel",)),
        name="norm_mod",
    )(x, g, shift, scale)


def _matmul_kernel(a_ref, w_ref, o_ref, *, act):
    acc = _dot(a_ref[...], w_ref[...])
    if act == "sigmoid":
        acc = jax.nn.sigmoid(acc)
    o_ref[...] = acc.astype(o_ref.dtype)


def matmul(a, w, out_dtype, act=None, tm=1024, tn=256, name="matmul"):
    m, k = a.shape
    n = w.shape[1]
    tm = min(tm, m)
    return pl.pallas_call(
        functools.partial(_matmul_kernel, act=act),
        grid=(m // tm, n // tn),
        in_specs=[pl.BlockSpec((tm, k), lambda i, j: (i, 0)), pl.BlockSpec((k, tn), lambda i, j: (0, j))],
        out_specs=pl.BlockSpec((tm, tn), lambda i, j: (i, j)),
        out_shape=jax.ShapeDtypeStruct((m, n), out_dtype),
        compiler_params=_cparams(("parallel", "arbitrary")),
        name=name,
    )(a, w)


def _softmax_parts(scores, extra=None):
    mx = scores[0].max(axis=-1, keepdims=True)
    for s in scores[1:]:
        mx = jnp.maximum(mx, s.max(axis=-1, keepdims=True))
    if extra is not None:
        mx = jnp.maximum(mx, extra)
    ps = [jnp.exp(s - mx) for s in scores]
    den = ps[0].sum(axis=-1, keepdims=True)
    for p in ps[1:]:
        den = den + p.sum(axis=-1, keepdims=True)
    if extra is not None:
        den = den + jnp.exp(extra - mx)
    return ps, den


def _head_rmsnorm(o, g, mult):
    r = lax.rsqrt(jnp.mean(o * o, axis=-1, keepdims=True) + EPS)
    return o * r * g * mult


def _ctx_mixers_kernel(s_ref, lam_ref, ng_ref, sink_ref, oa_ref, oc_ref, od_ref, *, lam_init):
    lam = lam_ref[...]
    for h in range(H_A):
        v = s_ref[:, 1024 + h * DV_A:1024 + (h + 1) * DV_A].astype(BF16)
        outs = []
        for mi in range(2):
            c0 = OFF_A + (h * 2 + mi) * DH_A
            q = (s_ref[:, c0:c0 + DH_A] * (DH_A ** -0.5)).astype(BF16)
            k = s_ref[:, 512 + c0:512 + c0 + DH_A].astype(BF16)
            (p,), den = _softmax_parts([_dot_nt(q, k)])
            outs.append(_dot(p.astype(BF16), v) / den)
        o = _head_rmsnorm(outs[0] - lam * outs[1], ng_ref[...], 1.0 - lam_init)
        oa_ref[:, h * DV_A:(h + 1) * DV_A] = o.astype(oa_ref.dtype)
    grp = H_C // KVH_C
    for h in range(H_C):
        n = h // grp
        q = (s_ref[:, OFF_C + h * DH_C:OFF_C + (h + 1) * DH_C] * (DH_C ** -0.5)).astype(BF16)
        k = s_ref[:, OFF_C + 512 + n * DH_C:OFF_C + 512 + (n + 1) * DH_C].astype(BF16)
        v = s_ref[:, OFF_C + 640 + n * DH_C:OFF_C + 640 + (n + 1) * DH_C].astype(BF16)
        sink = jnp.broadcast_to(sink_ref[:, h:h + 1], (q.shape[0], 1))
        (p,), den = _softmax_parts([_dot_nt(q, k)], extra=sink)
        o = _dot(p.astype(BF16), v) / den
        oc_ref[:, h * DH_C:(h + 1) * DH_C] = o.astype(oc_ref.dtype)
    for h in range(H_D):
        q = (s_ref[:, OFF_D + h * DH_D:OFF_D + (h + 1) * DH_D] * (DH_D ** -0.5)).astype(BF16)
        k = s_ref[:, OFF_D + 512 + h * DH_D:OFF_D + 512 + (h + 1) * DH_D].astype(BF16)
        v = s_ref[:, OFF_D + 1024 + h * DH_D:OFF_D + 1024 + (h + 1) * DH_D].astype(BF16)
        (p,), den = _softmax_parts([_dot_nt(q, k)])
        o = _dot(p.astype(BF16), v) / den
        od_ref[:, h * DH_D:(h + 1) * DH_D] = o.astype(od_ref.dtype)


def ctx_mixers(slab, lam, lam_init, norm_g, sink):
    b, s, _ = slab.shape
    o_spec = pl.BlockSpec((None, s, BRANCH_W), lambda i: (i, 0, 0))
    o_shape = jax.ShapeDtypeStruct((b, s, BRANCH_W), BF16)
    return pl.pallas_call(
        functools.partial(_ctx_mixers_kernel, lam_init=lam_init),
        grid=(b,),
        in_specs=[pl.BlockSpec((None, s, N_QKV), lambda i: (i, 0, 0)),
                  pl.BlockSpec((1, 1), lambda i: (0, 0)),
                  pl.BlockSpec((1, DV_A), lambda i: (0, 0)),
                  pl.BlockSpec((1, H_C), lambda i: (0, 0))],
        out_specs=[o_spec, o_spec, o_spec],
        out_shape=[o_shape, o_shape, o_shape],
        compiler_params=_cparams(("parallel",)),
        name="ctx_mixers",
    )(slab, lam, norm_g, sink)


def _rope128(x, cos, sin):
    lane = lax.broadcasted_iota(jnp.int32, x.shape, 1)
    other = jnp.where((lane % 64) < 32, pltpu.roll(x, 96, axis=1), pltpu.roll(x, 32, axis=1))
    return x * cos + other * sin


def _lat_prep_kernel(s_ref, cos_ref, sin_ref, a_ref, c_ref, d_ref):
    cos, sin = cos_ref[...], sin_ref[...]

    def put(dst, dst0, src0, rope, scale):
        x = s_ref[:, src0:src0 + 128]
        if rope:
            x = _rope128(x, cos, sin)
        if scale != 1.0:
            x = x * scale
        dst[:, dst0:dst0 + 128] = x.astype(dst.dtype)

    for j in range(12):
        put(a_ref, j * 128, OFF_A + j * 128, j < 8, DH_A ** -0.5 * LOG2E if j < 4 else 1.0)
    for j in range(6):
        if j < 4:
            put(c_ref, j * 128, OFF_C + j * 128, True, DH_C ** -0.5)
        elif j == 4:
            put(c_ref, j * 128, OFF_C + j * 128, True, 1.0)
        else:
            put(c_ref, j * 128, OFF_C + j * 128, False, 1.0)
    for j in range(12):
        put(d_ref, j * 128, OFF_D + j * 128, False, DH_D ** -0.5 if j < 4 else 1.0)


def lat_prep(slab, cos128, sin128, tm=512):
    b, n, _ = slab.shape
    tm = min(tm, n)
    widths = (1536, 768, 1536)
    return pl.pallas_call(
        _lat_prep_kernel,
        grid=(b, n // tm),
        in_specs=[pl.BlockSpec((None, tm, N_QKV), lambda i, j: (i, j, 0)),
                  pl.BlockSpec((tm, 128), lambda i, j: (j, 0)),
                  pl.BlockSpec((tm, 128), lambda i, j: (j, 0))],
        out_specs=[pl.BlockSpec((None, tm, w), lambda i, j: (i, j, 0)) for w in widths],
        out_shape=[jax.ShapeDtypeStruct((b, n, w), BF16) for w in widths],
        compiler_params=_cparams(("parallel", "parallel")),
        name="lat_prep",
    )(slab, cos128, sin128)


def _lat_a_kernel(q_ref, k_ref, v_ref, kc_ref, vc_ref, lam_ref, ng_ref, o_ref, *, tk, lam_init):
    n = k_ref.shape[0]
    tq = q_ref.shape[0]
    q = q_ref[...]
    qs = (q[:, :DH_A], q[:, DH_A:])

    def update(state, kblk, vblk):
        new = []
        for mi in range(2):
            m_old, l_old, acc = state[mi]
            s = _dot_nt(qs[mi], kblk[:, mi * DH_A:(mi + 1) * DH_A])
            m_new = jnp.maximum(m_old, s.max(axis=-1, keepdims=True))
            alpha = jnp.exp2(m_old - m_new)
            p = jnp.exp2(s - m_new)
            l_new = alpha * l_old + p.sum(axis=-1, keepdims=True)
            acc = alpha * acc + _dot(p.astype(BF16), vblk)
            new.append((m_new, l_new, acc))
        return tuple(new)

    init = tuple((jnp.full((tq, 1), -jnp.inf, F32), jnp.zeros((tq, 1), F32), jnp.zeros((tq, DV_A), F32))
                 for _ in range(2))
    state = update(init, kc_ref[...].astype(BF16), vc_ref[...].astype(BF16))

    def body(j, st):
        r0 = pl.multiple_of(j * tk, tk)
        return update(st, k_ref[pl.ds(r0, tk), :], v_ref[pl.ds(r0, tk), :])

    state = lax.fori_loop(0, n // tk, body, state)
    (_, l1, a1), (_, l2, a2) = state
    o = a1 / l1 - lam_ref[...] * (a2 / l2)
    o_ref[...] = _head_rmsnorm(o, ng_ref[...], 1.0 - lam_init).astype(o_ref.dtype)


def lat_attn_a(a_qkv, ctx_k, ctx_v, layer, lam, lam_init, norm_g, tq=1024, tk=512):
    b, n, _ = a_qkv.shape
    past = ctx_k.shape[2]
    tq, tk = min(tq, n), min(tk, n)
    return pl.pallas_call(
        functools.partial(_lat_a_kernel, tk=tk, lam_init=lam_init),
        grid=(b, H_A, n // tq),
        in_specs=[pl.BlockSpec((None, tq, 128), lambda i, h, j: (i, j, h)),
                  pl.BlockSpec((None, n, 128), lambda i, h, j: (i, 0, H_A + h)),
                  pl.BlockSpec((None, n, 128), lambda i, h, j: (i, 0, 2 * H_A + h)),
                  pl.BlockSpec((None, None, past, 128), lambda i, h, j: (i, layer, 0, h)),
                  pl.BlockSpec((None, None, past, 128), lambda i, h, j: (i, layer, 0, h)),
                  pl.BlockSpec((1, 1), lambda i, h, j: (0, 0)),
                  pl.BlockSpec((1, DV_A), lambda i, h, j: (0, 0))],
        out_specs=pl.BlockSpec((None, tq, DV_A), lambda i, h, j: (i, j, h)),
        out_shape=jax.ShapeDtypeStruct((b, n, BRANCH_W), BF16),
        compiler_params=_cparams(("parallel", "parallel", "arbitrary")),
        name="lat_attn_a",
    )(a_qkv, a_qkv, a_qkv, ctx_k, ctx_v, lam, norm_g)


def _lat_c_kernel(q_ref, kv_ref, kc_ref, vc_ref, sink_ref, o_ref):
    blk = q_ref.shape[0]
    n_tok = kv_ref.shape[0]
    span = 3 * blk
    i = pl.program_id(1)
    start = pl.multiple_of(jnp.clip((i - 1) * blk, 0, n_tok - span), blk)
    kv = kv_ref[pl.ds(start, span), :]
    kc = kc_ref[...].astype(BF16)
    vc = vc_ref[...].astype(BF16)
    grp = H_C // KVH_C
    qpos = i * blk + lax.broadcasted_iota(jnp.int32, (grp * blk, span), 0) % blk
    kpos = start + lax.broadcasted_iota(jnp.int32, (grp * blk, span), 1)
    band = jnp.abs(kpos - qpos) <= WINDOW_C
    for n in range(KVH_C):
        heads = range(n * grp, (n + 1) * grp)
        q = jnp.concatenate([q_ref[:, h * DH_C:(h + 1) * DH_C] for h in heads], axis=0)
        sink = jnp.concatenate([jnp.broadcast_to(sink_ref[:, h:h + 1], (blk, 1)) for h in heads], axis=0)
        s_loc = jnp.where(band, _dot_nt(q, kv[:, n * DH_C:(n + 1) * DH_C]), NEG_INF)
        s_ctx = _dot_nt(q, kc[:, n * DH_C:(n + 1) * DH_C])
        (p_loc, p_ctx), den = _softmax_parts([s_loc, s_ctx], extra=sink)
        v_loc = kv[:, KVH_C * DH_C + n * DH_C:KVH_C * DH_C + (n + 1) * DH_C]
        o = (_dot(p_loc.astype(BF16), v_loc) + _dot(p_ctx.astype(BF16), vc[:, n * DH_C:(n + 1) * DH_C])) / den
        for g, h in enumerate(heads):
            o_ref[:, h * DH_C:(h + 1) * DH_C] = o[g * blk:(g + 1) * blk].astype(o_ref.dtype)


def lat_attn_c(c_qkv, ctx_k, ctx_v, layer, sink):
    b, n, _ = c_qkv.shape
    past = ctx_k.shape[2]
    blk = WINDOW_C
    assert n % blk == 0 and n >= 3 * blk
    return pl.pallas_call(
        _lat_c_kernel,
        grid=(b, n // blk),
        in_specs=[pl.BlockSpec((None, blk, 512), lambda i, j: (i, j, 0)),
                  pl.BlockSpec((None, n, 256), lambda i, j: (i, 0, 2)),
                  pl.BlockSpec((None, None, past, 128), lambda i, j: (i, layer, 0, 0)),
                  pl.BlockSpec((None, None, past, 128), lambda i, j: (i, layer, 0, 0)),
                  pl.BlockSpec((1, H_C), lambda i, j: (0, 0))],
        out_specs=pl.BlockSpec((None, blk, BRANCH_W), lambda i, j: (i, j, 0)),
        out_shape=jax.ShapeDtypeStruct((b, n, BRANCH_W), BF16),
        compiler_params=_cparams(("parallel", "arbitrary")),
        name="lat_attn_c",
    )(c_qkv, c_qkv, ctx_k, ctx_v, sink)


D_ROWS = 4


def _nbr_geometry(rows):
    wr = min(WIN_R, rows)
    rblk = min(D_ROWS, rows)
    span = min(rblk + wr - 1, rows)
    bases, cls, patterns, seen = [], [], [], {}
    for rb in range(rows // rblk):
        r = rb * rblk + np.arange(rblk)
        r0 = np.clip(r - wr // 2, 0, rows - wr)
        base = int(np.clip(r0.min(), 0, rows - span))
        kr = base + np.arange(span)
        valid = (kr[None, :] >= r0[:, None]) & (kr[None, :] < r0[:, None] + wr)
        assert valid.sum(axis=1).min() == wr
        dr_idx = np.where(valid, kr[None, :] - r[:, None] + WIN_R - 1, 0)
        key = dr_idx.tobytes() + valid.tobytes()
        if key not in seen:
            seen[key] = len(patterns)
            patterns.append((dr_idx, valid))
        bases.append(base)
        cls.append(seen[key])
    return wr, rblk, span, np.asarray(bases, np.int32), np.asarray(cls, np.int32), patterns


def _lat_d_kernel(base_ref, cls_ref, q_ref, k_ref, v_ref, kc_ref, vc_ref, bias_ref, o_ref, *, span):
    t0 = pl.multiple_of(base_ref[pl.program_id(1)] * GRID_W, GRID_W)
    kb = k_ref[pl.ds(t0, span * GRID_W), :]
    vb = v_ref[pl.ds(t0, span * GRID_W), :]
    kc = kc_ref[...].astype(BF16)
    vc = vc_ref[...].astype(BF16)
    for h in range(H_D):
        sl = slice(h * DH_D, (h + 1) * DH_D)
        q = q_ref[:, sl]
        s_loc = _dot_nt(q, kb[:, sl]) + bias_ref[h]
        s_ctx = _dot_nt(q, kc[:, sl])
        (p_loc, p_ctx), den = _softmax_parts([s_loc, s_ctx])
        o = (_dot(p_loc.astype(BF16), vb[:, sl]) + _dot(p_ctx.astype(BF16), vc[:, sl])) / den
        o_ref[:, sl] = o.astype(o_ref.dtype)


def lat_attn_d(d_qkv, ctx_k, ctx_v, layer, bias_tab):
    b, n, _ = d_qkv.shape
    past = ctx_k.shape[2]
    rows = n // GRID_W
    _, rblk, span, base, cls, _ = _nbr_geometry(rows)
    nq = rblk * GRID_W
    grid_spec = pltpu.PrefetchScalarGridSpec(
        num_scalar_prefetch=2,
        grid=(b, rows // rblk),
        in_specs=[pl.BlockSpec((None, nq, 512), lambda i, r, bs, cl: (i, r, 0)),
                  pl.BlockSpec((None, n, 512), lambda i, r, bs, cl: (i, 0, 1)),
                  pl.BlockSpec((None, n, 512), lambda i, r, bs, cl: (i, 0, 2)),
                  pl.BlockSpec((None, None, past, 512), lambda i, r, bs, cl: (i, layer, 0, 0)),
                  pl.BlockSpec((None, None, past, 512), lambda i, r, bs, cl: (i, layer, 0, 0)),
                  pl.BlockSpec((None, H_D, nq, span * GRID_W), lambda i, r, bs, cl: (cl[r], 0, 0, 0))],
        out_specs=pl.BlockSpec((None, nq, BRANCH_W), lambda i, r, bs, cl: (i, r, 0)),
    )
    return pl.pallas_call(
        functools.partial(_lat_d_kernel, span=span),
        grid_spec=grid_spec,
        out_shape=jax.ShapeDtypeStruct((b, n, BRANCH_W), BF16),
        compiler_params=_cparams(("parallel", "arbitrary")),
        name="lat_attn_d",
    )(jnp.asarray(base), jnp.asarray(cls), d_qkv, d_qkv, d_qkv, ctx_k, ctx_v, bias_tab)


def neighbourhood_bias(rpb, rows):
    _, rblk, span, _, _, patterns = _nbr_geometry(rows)
    cols = np.arange(GRID_W)
    cstart = np.clip(cols - WIN_C // 2, 0, GRID_W - WIN_C)
    inside_c = (cols[None, :] >= cstart[:, None]) & (cols[None, :] < cstart[:, None] + WIN_C)
    dc_idx = np.clip(cols[None, :] - cols[:, None] + WIN_C - 1, 0, 2 * WIN_C - 2)
    rp = rpb.astype(F32)
    tabs = []
    for dr_idx, valid in patterns:
        tab = rp[:, dr_idx][:, :, :, dc_idx]
        ok = valid[None, :, :, None, None] & inside_c[None, None, None]
        tab = jnp.where(ok, tab, NEG_INF)
        tab = jnp.transpose(tab, (0, 1, 3, 2, 4))
        tabs.append(tab.reshape(H_D, rblk * GRID_W, span * GRID_W))
    return jnp.stack(tabs)


def ssm_matrices(lam_re, lam_im, log_dt, b_re, b_im, c_re, c_im):
    t = SSM_T
    out = []
    for d in range(2):
        dt = jnp.exp(log_dt[d].astype(F32))[:, None]
        lr, li = lam_re[d].astype(F32), lam_im[d].astype(F32)
        zr, zi = lr * dt, li * dt

        def lam_pow(tau):
            mag = jnp.exp(zr * tau)
            return mag * jnp.cos(zi * tau), mag * jnp.sin(zi * tau)

        lbr, lbi = lam_pow(1.0)
        den = lr * lr + li * li
        nr, ni = lbr - 1.0, lbi
        qr, qi = (nr * lr + ni * li) / den, (ni * lr - nr * li) / den
        br, bi = b_re[d].astype(F32), b_im[d].astype(F32)
        bbr = qr[..., None] * br - qi[..., None] * bi
        bbi = qr[..., None] * bi + qi[..., None] * br
        cr, ci = c_re[d].astype(F32), c_im[d].astype(F32)
        taus = jnp.arange(t + 1, dtype=F32)
        pw = [lam_pow(taus[i]) for i in range(t + 1)]
        ar = jnp.stack([p[0] for p in pw])
        ai = jnp.stack([p[1] for p in pw])
        pr = ar[:t, :, :, None] * bbr[None] - ai[:t, :, :, None] * bbi[None]
        pi = ar[:t, :, :, None] * bbi[None] + ai[:t, :, :, None] * bbr[None]
        kk = (jnp.einsum('gep,tgpc->tgec', cr, pr, precision=HIGHEST)
              - jnp.einsum('gep,tgpc->tgec', ci, pi, precision=HIGHEST))
        jj = np.arange(t)[:, None]
        tt = np.arange(t)[None, :]
        lag = (tt - jj) if d == 0 else (jj - tt)
        valid = lag >= 0
        m = kk[np.clip(lag, 0, t - 1)]
        m = jnp.where(valid[:, :, None, None, None], m, 0.0)
        m = jnp.transpose(m, (2, 0, 4, 1, 3)).reshape(G_B, t * CG_B, t * CG_B)
        epow = (t - 1 - np.arange(t)) if d == 0 else np.arange(t)
        er = jnp.transpose(pr[epow], (1, 0, 3, 2)).reshape(G_B, t * CG_B, P_B)
        ei = jnp.transpose(pi[epow], (1, 0, 3, 2)).reshape(G_B, t * CG_B, P_B)
        fpow = (np.arange(t) + 1) if d == 0 else (t - np.arange(t))
        far, fai = ar[fpow], ai[fpow]
        fr = cr[None] * far[:, :, None, :] - ci[None] * fai[:, :, None, :]
        fi = -(cr[None] * fai[:, :, None, :] + ci[None] * far[:, :, None, :])
        fr = jnp.transpose(fr, (1, 3, 0, 2)).reshape(G_B, P_B, t * CG_B)
        fi = jnp.transpose(fi, (1, 3, 0, 2)).reshape(G_B, P_B, t * CG_B)
        out.append((m, er, ei, fr, fi, ar[t], ai[t]))
    return out


def _ssm_kernel(u_ref, mf_ref, mb_ref, efr_ref, efi_ref, ebr_ref, ebi_ref, ffr_ref, ffi_ref, fbr_ref, fbi_ref,
                dr_ref, di_ref, s0r_ref, s0i_ref, y_ref, sr_out, si_out, xr_s, xi_s, cr_s, ci_s, *, nb):
    nc, b2, p = xr_s.shape
    u = u_ref[...]
    rid = lax.broadcasted_iota(jnp.int32, (nc * b2, 1), 0)
    isf = (rid % b2) < nb
    xr_s[...] = jnp.where(isf, _dot(u, efr_ref[...]), _dot(u, ebr_ref[...])).reshape(nc, b2, p)
    xi_s[...] = jnp.where(isf, _dot(u, efi_ref[...]), _dot(u, ebi_ref[...])).reshape(nc, b2, p)
    dr, di = dr_ref[...], di_ref[...]

    def body(k, carry):
        sr, si = carry
        cr_s[k] = sr
        ci_s[k] = si
        return dr * sr - di * si + xr_s[k], dr * si + di * sr + xi_s[k]

    sr, si = lax.fori_loop(0, nc, body, (s0r_ref[...], s0i_ref[...]))
    sr_out[...] = sr
    si_out[...] = si
    cr = cr_s[...].reshape(nc * b2, p).astype(BF16)
    ci = ci_s[...].reshape(nc * b2, p).astype(BF16)
    yf = _dot(u, mf_ref[...]) + _dot(cr, ffr_ref[...]) + _dot(ci, ffi_ref[...])
    yb = _dot(u, mb_ref[...]) + _dot(cr, fbr_ref[...]) + _dot(ci, fbi_ref[...])
    y_ref[...] = jnp.where(isf, yf, yb)


def ssm_scan(u, mats, s0_re, s0_im):
    nb, s, _ = u.shape
    t = SSM_T
    nc = s // t
    b2 = 2 * nb
    (mf, efr, efi, ffr, ffi, dfr, dfi), (mb, ebr, ebi, fbr, fbi, dbr, dbi) = mats
    ug = u.astype(BF16).reshape(nb, nc, t, G_B, CG_B)
    ug = jnp.transpose(ug, (3, 1, 0, 2, 4)).reshape(G_B, nc, nb, t * CG_B)
    u2 = jnp.concatenate([ug, ug[:, ::-1]], axis=2).reshape(G_B, nc * b2, t * CG_B)

    def per_row(f, bk):
        return jnp.concatenate([jnp.broadcast_to(f[:, None], (G_B, nb, P_B)),
                                jnp.broadcast_to(bk[:, None], (G_B, nb, P_B))], axis=1)

    dr, di = per_row(dfr, dbr), per_row(dfi, dbi)
    s0r = jnp.transpose(s0_re.astype(F32), (2, 1, 0, 3)).reshape(G_B, b2, P_B)
    s0i = jnp.transpose(s0_im.astype(F32), (2, 1, 0, 3)).reshape(G_B, b2, P_B)
    bf = lambda a: a.astype(BF16)
    g3 = lambda a, c: pl.BlockSpec((None, a, c), lambda g: (g, 0, 0))
    tc = t * CG_B
    y2, sr, si = pl.pallas_call(
        functools.partial(_ssm_kernel, nb=nb),
        grid=(G_B,),
        in_specs=[g3(nc * b2, tc), g3(tc, tc), g3(tc, tc)] + [g3(tc, P_B)] * 4 + [g3(P_B, tc)] * 4
        + [g3(b2, P_B)] * 4,
        out_specs=[g3(nc * b2, tc), g3(b2, P_B), g3(b2, P_B)],
        out_shape=[jax.ShapeDtypeStruct((G_B, nc * b2, tc), F32),
                   jax.ShapeDtypeStruct((G_B, b2, P_B), F32), jax.ShapeDtypeStruct((G_B, b2, P_B), F32)],
        scratch_shapes=[pltpu.VMEM((nc, b2, P_B), F32) for _ in range(4)],
        compiler_params=_cparams(("parallel",)),
        name="ssm_scan",
    )(u2, bf(mf), bf(mb), bf(efr), bf(efi), bf(ebr), bf(ebi), bf(ffr), bf(ffi), bf(fbr), bf(fbi),
      dr, di, s0r, s0i)
    y2 = y2.reshape(G_B, nc, b2, t, CG_B)
    y = y2[:, :, :nb] + y2[:, ::-1, nb:]
    y = jnp.transpose(y, (2, 1, 3, 0, 4)).reshape(nb, s, W_B)
    fin_r = jnp.transpose(sr.reshape(G_B, 2, nb, P_B), (2, 1, 0, 3))
    fin_i = jnp.transpose(si.reshape(G_B, 2, nb, P_B), (2, 1, 0, 3))
    return y, fin_r, fin_i


def _merge_kernel(oa_ref, oc_ref, od_ref, u_ref, ys_ref, gate_ref, dsk_ref, wglu_ref, wb_ref, o_ref):
    y = jax.nn.gelu(u_ref[...] * dsk_ref[...] + ys_ref[...])
    z = _dot(y.astype(BF16), wglu_ref[...])
    ob = (z[:, :W_B] * jax.nn.sigmoid(z[:, W_B:])).astype(BF16)
    acc = None
    for i, o in enumerate((oa_ref[...], ob, oc_ref[...], od_ref[...])):
        term = gate_ref[:, i * D_MODEL:(i + 1) * D_MODEL].astype(F32) * _dot(o, wb_ref[i])
        acc = term if acc is None else acc + term
    o_ref[...] = acc.astype(o_ref.dtype)


def branch_merge(o_a, o_c, o_d, slab, y_ssm, gates, ssm_d, w_glu, w_branch, tm=256):
    m = o_a.shape[0]
    tm = min(tm, m)
    row = lambda w: pl.BlockSpec((tm, w), lambda i: (i, 0))
    return pl.pallas_call(
        _merge_kernel,
        grid=(m // tm,),
        in_specs=[row(BRANCH_W), row(BRANCH_W), row(BRANCH_W),
                  pl.BlockSpec((tm, W_B), lambda i: (i, OFF_U // W_B)),
                  row(W_B), row(N_BRANCH * D_MODEL),
                  pl.BlockSpec((1, W_B), lambda i: (0, 0)),
                  pl.BlockSpec((W_B, 2 * W_B), lambda i: (0, 0)),
                  pl.BlockSpec((N_BRANCH, BRANCH_W, D_MODEL), lambda i: (0, 0, 0))],
        out_specs=row(D_MODEL),
        out_shape=jax.ShapeDtypeStruct((m, D_MODEL), BF16),
        compiler_params=_cparams(("parallel",)),
        name="branch_merge",
    )(o_a, o_c, o_d, slab, y_ssm, gates, ssm_d, w_glu, w_branch)


def _route_select(sc, bi):
    a = [bi[j * N_GROUPS:(j + 1) * N_GROUPS] for j in range(EXPERTS_PER_GROUP)]
    s = [sc[j * N_GROUPS:(j + 1) * N_GROUPS] for j in range(EXPERTS_PER_GROUP)]

    def first_eq(v, m):
        return jnp.where(v[0] == m, 0, jnp.where(v[1] == m, 1, jnp.where(v[2] == m, 2, 3)))

    def by_slot(v, i):
        return jnp.where(i == 0, v[0], jnp.where(i == 1, v[1], jnp.where(i == 2, v[2], v[3])))

    m1 = jnp.maximum(jnp.maximum(a[0], a[1]), jnp.maximum(a[2], a[3]))
    i1 = first_eq(a, m1)
    b = [jnp.where(i1 == j, -jnp.inf, a[j]) for j in range(EXPERTS_PER_GROUP)]
    m2 = jnp.maximum(jnp.maximum(b[0], b[1]), jnp.maximum(b[2], b[3]))
    i2 = first_eq(b, m2)
    gs = m1 + m2
    grow = lax.broadcasted_iota(jnp.int32, gs.shape, 0)
    gsel = jnp.min(jnp.where(gs == jnp.max(gs, axis=0, keepdims=True), grow, N_GROUPS), axis=0, keepdims=True)
    oh = grow == gsel
    pick_i = lambda v: jnp.sum(jnp.where(oh, v, 0), axis=0, keepdims=True)
    pick_f = lambda v: jnp.sum(jnp.where(oh, v, 0.0), axis=0, keepdims=True)
    return gsel, pick_i(i1), pick_i(i2), pick_f(by_slot(s, i1)), pick_f(by_slot(s, i2))


def _out_kernel(x_ref, mg_ref, w_ref, g1_ref, n2_ref, sh_ref, sc_ref, rw_ref, rb_ref,
                xo_ref, h_ref, ri_ref, rwt_ref, cnt_ref, run_s):
    @pl.when(pl.program_id(0) == 0)
    def _():
        run_s[...] = jnp.zeros_like(run_s)

    x = x_ref[...] + g1_ref[...] * _dot(mg_ref[...], w_ref[...])
    xo_ref[...] = x
    r = lax.rsqrt(jnp.mean(x * x, axis=-1, keepdims=True) + EPS)
    h = (x * r * n2_ref[...]) * (1.0 + sc_ref[...]) + sh_ref[...]
    h_ref[...] = h.astype(h_ref.dtype)
    tm = h.shape[0]
    logits = lax.dot_general(rw_ref[...], h, (((1,), (1,)), ((), ())), precision=HIGHEST,
                             preferred_element_type=F32)
    sc = jax.nn.sigmoid(logits)
    gsel, l1, l2, s1, s2 = _route_select(sc, sc + rb_ref[...])
    den = s1 + s2
    rowid = lax.broadcasted_iota(jnp.int32, (N_EXPERTS, tm), 0)
    oh1 = rowid == l1 * N_GROUPS + gsel
    oh2 = rowid == l2 * N_GROUPS + gsel
    ohf = jnp.where(jnp.logical_or(oh1, oh2), 1.0, 0.0)
    before = lax.broadcasted_iota(jnp.int32, (tm, tm), 0) < lax.broadcasted_iota(jnp.int32, (tm, tm), 1)
    prefix = _dot(ohf.astype(BF16), jnp.where(before, 1.0, 0.0).astype(BF16)) + run_s[:, 0:1]
    rank1 = jnp.sum(jnp.where(oh1, prefix, 0.0), axis=0, keepdims=True)
    rank2 = jnp.sum(jnp.where(oh2, prefix, 0.0), axis=0, keepdims=True)
    run_s[...] = run_s[...] + jnp.sum(ohf, axis=1, keepdims=True)
    cnt_ref[...] = run_s[...]
    e1 = gsel * EXPERTS_PER_GROUP + l1
    e2 = gsel * EXPERTS_PER_GROUP + l2
    ri_ref[...] = jnp.concatenate([e1, e2, rank1.astype(jnp.int32), rank2.astype(jnp.int32),
                                   jnp.zeros((4, tm), jnp.int32)], axis=0)
    rwt_ref[...] = jnp.concatenate([s1 / den, s2 / den, jnp.zeros((6, tm), F32)], axis=0)


def out_proj(x, merged, w_out, gate1, norm2_g, shift2, scale2, router_wt, router_bt, rows_per_mod, tm=256):
    m, d = x.shape
    tm = min(tm, rows_per_mod)
    per = rows_per_mod // tm
    mod = pl.BlockSpec((None, 1, d), lambda i: (i // per, 0, 0))
    row = lambda w: pl.BlockSpec((tm, w), lambda i: (i, 0))
    col = pl.BlockSpec((8, tm), lambda i: (0, i))
    return pl.pallas_call(
        _out_kernel,
        grid=(m // tm,),
        in_specs=[row(d), row(d), pl.BlockSpec((d, d), lambda i: (0, 0)), mod,
                  pl.BlockSpec((1, d), lambda i: (0, 0)), mod, mod,
                  pl.BlockSpec((N_EXPERTS, d), lambda i: (0, 0)),
                  pl.BlockSpec((N_EXPERTS, 1), lambda i: (0, 0))],
        out_specs=[row(d), row(d), col, col, pl.BlockSpec((N_EXPERTS, 128), lambda i: (0, 0))],
        out_shape=[jax.ShapeDtypeStruct((m, d), F32), jax.ShapeDtypeStruct((m, d), BF16),
                   jax.ShapeDtypeStruct((8, m), jnp.int32), jax.ShapeDtypeStruct((8, m), F32),
                   jax.ShapeDtypeStruct((N_EXPERTS, 128), F32)],
        scratch_shapes=[pltpu.VMEM((N_EXPERTS, 128), F32)],
        compiler_params=_cparams(("arbitrary",)),
        name="out_proj",
    )(x, merged, w_out, gate1, norm2_g, shift2, scale2, router_wt, router_bt)


def _experts_kernel(be_ref, nu_ref, x_ref, w1_ref, w3_ref, w2_ref, o_ref, w1_s, w3_s, w2_s):
    i = pl.program_id(0)
    changed = jnp.logical_or(i == 0, be_ref[i] != be_ref[jnp.maximum(i - 1, 0)])

    @pl.when(jnp.logical_and(changed, i < nu_ref[0]))
    def _():
        w1_s[...] = w1_ref[...].astype(BF16)
        w3_s[...] = w3_ref[...].astype(BF16)
        w2_s[...] = w2_ref[...].astype(BF16)

    @pl.when(i < nu_ref[0])
    def _():
        x = x_ref[...]
        a = _dot(x, w1_s[...])
        hid = (a * jax.nn.sigmoid(a)) * _dot(x, w3_s[...])
        o_ref[...] = _dot(hid.astype(BF16), w2_s[...]).astype(o_ref.dtype)

    @pl.when(i >= nu_ref[0])
    def _():
        o_ref[...] = jnp.zeros_like(o_ref)


def experts(buf, block_e, n_used, w1, w3, w2, layer):
    rows, d = buf.shape
    f = w1.shape[-1]
    nblk = rows // MOE_ROWS
    grid_spec = pltpu.PrefetchScalarGridSpec(
        num_scalar_prefetch=2,
        grid=(nblk,),
        in_specs=[pl.BlockSpec((MOE_ROWS, d), lambda i, be, nu: (i, 0)),
                  pl.BlockSpec((None, None, d, f), lambda i, be, nu: (layer, be[i], 0, 0)),
                  pl.BlockSpec((None, None, d, f), lambda i, be, nu: (layer, be[i], 0, 0)),
                  pl.BlockSpec((None, None, f, d), lambda i, be, nu: (layer, be[i], 0, 0))],
        out_specs=pl.BlockSpec((MOE_ROWS, d), lambda i, be, nu: (i, 0)),
        scratch_shapes=[pltpu.VMEM((d, f), BF16), pltpu.VMEM((d, f), BF16), pltpu.VMEM((f, d), BF16)],
    )
    return pl.pallas_call(
        _experts_kernel,
        grid_spec=grid_spec,
        out_shape=jax.ShapeDtypeStruct((rows, d), BF16),
        compiler_params=_cparams(("arbitrary",)),
        name="experts",
    )(block_e, n_used, buf, w1, w3, w2)


def _combine_kernel(x_ref, y0_ref, y1_ref, w_ref, g2_ref, fg_ref, o_ref, *, final):
    w = w_ref[...]
    y = y0_ref[...].astype(F32) * w[:, 0:1] + y1_ref[...].astype(F32) * w[:, 1:2]
    x = x_ref[...] + g2_ref[...] * y
    if final:
        r = lax.rsqrt(jnp.mean(x * x, axis=-1, keepdims=True) + EPS)
        x = x * r * fg_ref[...]
    o_ref[...] = x


def moe_combine(x, y0, y1, wts, gate2, final_g, rows_per_mod, final, tm=512):
    m, d = x.shape
    tm = min(tm, rows_per_mod)
    per = rows_per_mod // tm
    row = lambda w: pl.BlockSpec((tm, w), lambda i: (i, 0))
    return pl.pallas_call(
        functools.partial(_combine_kernel, final=final),
        grid=(m // tm,),
        in_specs=[row(d), row(d), row(d), row(TOP_K),
                  pl.BlockSpec((None, 1, d), lambda i: (i // per, 0, 0)),
                  pl.BlockSpec((1, d), lambda i: (0, 0))],
        out_specs=row(d),
        out_shape=jax.ShapeDtypeStruct((m, d), F32),
        compiler_params=_cparams(("parallel",)),
        name="moe_combine",
    )(x, y0, y1, wts, gate2, final_g)


def moe_dispatch(ri, cnt):
    t = ri.shape[1]
    counts = cnt[:, 0].astype(jnp.int32).reshape(EXPERTS_PER_GROUP, N_GROUPS).T.reshape(N_EXPERTS)
    padded = (counts + MOE_ROWS - 1) // MOE_ROWS * MOE_ROWS
    ends_p = jnp.cumsum(padded)
    starts_p = ends_p - padded
    dest = starts_p[ri[0:2]] + ri[2:4]
    n_blocks = -(-(t * TOP_K) // MOE_ROWS) + N_EXPERTS
    tok = jnp.arange(t, dtype=jnp.int32)
    src_tok = jnp.zeros((n_blocks * MOE_ROWS,), jnp.int32).at[dest.reshape(-1)].set(jnp.concatenate([tok, tok]))
    block_e = jnp.minimum(jnp.searchsorted(ends_p, jnp.arange(n_blocks, dtype=jnp.int32) * MOE_ROWS, side='right'),
                          N_EXPERTS - 1).astype(jnp.int32)
    n_used = (ends_p[-1:] // MOE_ROWS).astype(jnp.int32)
    return src_tok, dest, block_e, n_used


def rope_tables(n_tok):
    t = jnp.arange(n_tok)
    row = (t // GRID_W).astype(F32)
    col = (t % GRID_W).astype(F32)
    quarter = DH_A // 4
    inv = ROPE_BASE ** (-jnp.arange(quarter, dtype=F32) / quarter)
    ang = jnp.concatenate([row[:, None] * inv, col[:, None] * inv], axis=-1)
    cos, sin = jnp.cos(ang), jnp.sin(ang)
    cos128 = jnp.tile(jnp.concatenate([cos, cos], axis=-1), (1, 2))
    sin128 = jnp.tile(jnp.concatenate([-sin, sin], axis=-1), (1, 2))
    return cos128, sin128


def trunk_layer(x, mod, lw, layer, ctx, final):
    b, n, d = x.shape
    m = b * n
    shift1, scale1, gate1, shift2, scale2, gate2 = mod
    rows_per_mod = m if shift1.shape[0] == 1 else n
    xf = x.reshape(m, d)
    h = norm_mod(xf, lw['norm1_g'], shift1, scale1, rows_per_mod)
    slab = matmul(h, lw['w_qkv'], F32, tn=256, name="in_proj_qkv")
    gates = matmul(h, lw['w_gate'], BF16, act="sigmoid", tn=512, name="in_proj_gate")
    slab3 = slab.reshape(b, n, N_QKV)
    u = slab3[:, :, OFF_U:OFF_U + W_B]
    lam, lam_init = lw['lam'], lw['lam_init']
    if ctx is None:
        o_a, o_c, o_d = ctx_mixers(slab3, lam, lam_init, lw['diff_norm_g'], lw['sink_c'])
        s0 = jnp.zeros((b, 2, G_B, P_B), F32)
        y_ssm, fin_r, fin_i = ssm_scan(u, lw['ssm_mats'], s0, s0)
        new_ctx = (slab3[:, :, 512:1024].reshape(b, n, H_A, 2, DH_A),
                   slab3[:, :, 1024:1536].reshape(b, n, H_A, DV_A),
                   fin_r, fin_i,
                   slab3[:, :, OFF_C + 512:OFF_C + 640].reshape(b, n, KVH_C, DH_C),
                   slab3[:, :, OFF_C + 640:OFF_C + 768].reshape(b, n, KVH_C, DH_C),
                   slab3[:, :, OFF_D + 512:OFF_D + 1024].reshape(b, n, H_D, DH_D),
                   slab3[:, :, OFF_D + 1024:OFF_D + 1536].reshape(b, n, H_D, DH_D))
    else:
        ak, av, s_re, s_im, ck, cv, dk, dv = ctx
        a_qkv, c_qkv, d_qkv = lat_prep(slab3, lw['cos128'], lw['sin128'])
        o_a = lat_attn_a(a_qkv, ak, av, layer, lam, lam_init, lw['diff_norm_g'])
        o_c = lat_attn_c(c_qkv, ck, cv, layer, lw['sink_c'])
        o_d = lat_attn_d(d_qkv, dk, dv, layer, lw['bias_d'])
        y_ssm, _, _ = ssm_scan(u, lw['ssm_mats'], s_re[:, layer], s_im[:, layer])
        new_ctx = None
    merged = branch_merge(o_a.reshape(m, -1), o_c.reshape(m, -1), o_d.reshape(m, -1), slab,
                          y_ssm.reshape(m, W_B), gates, lw['ssm_d'], lw['w_glu'], lw['w_branch'])
    x1, h2, ri, rw, cnt = out_proj(xf, merged, lw['w_out'], gate1, lw['norm2_g'], shift2, scale2,
                                   lw['router_wt'], lw['router_bt'], rows_per_mod)
    src_tok, dest, block_e, n_used = moe_dispatch(ri, cnt)
    yb = experts(h2[src_tok], block_e, n_used, lw['w_e1'], lw['w_e3'], lw['w_e2'], layer)
    x2 = moe_combine(x1, yb[dest[0]], yb[dest[1]], rw[0:2].T, gate2, lw['final_g'], rows_per_mod, final)
    return x2.reshape(b, n, d), new_ctx


def kernel(x_prompt, x_sample, cache_a_k, cache_a_v, state_ssm_re, state_ssm_im, cache_c_k, cache_c_v, cache_d_k, cache_d_v, c, c_ctx, w_mod, b_mod, norm1_g, norm2_g, w_in, diff_lambda, diff_norm_g, ssm_lam_re, ssm_lam_im, ssm_log_dt, ssm_b_re, ssm_b_im, ssm_c_re, ssm_c_im, ssm_d, ssm_w_glu, sink_c, rpb_d, w_branch, w_out, router_w, router_b, w_e1, w_e3, w_e2, final_g):
    depth = w_in.shape[0]
    db, dn, d = x_sample.shape
    past = cache_a_k.shape[2]
    cond = jnp.concatenate([c_ctx[None, :], c], axis=0).astype(F32)
    n_cond = cond.shape[0]
    cond = jnp.pad(cond, ((0, -n_cond % 8), (0, 0)))
    mod_all = adaln_all(cond, w_mod, b_mod[:, None, :])
    cos128, sin128 = rope_tables(dn)
    perm = (np.arange(N_GROUPS)[None, :] * EXPERTS_PER_GROUP + np.arange(EXPERTS_PER_GROUP)[:, None]).reshape(-1)
    router_wt = router_w.astype(F32).T[perm]
    router_bt = router_b.astype(F32)[perm][:, None]
    ctx_caches = (cache_a_k.reshape(db, depth, past, H_A * 2 * DH_A), cache_a_v.reshape(db, depth, past, H_A * DV_A),
                  state_ssm_re, state_ssm_im,
                  cache_c_k.reshape(db, depth, past, KVH_C * DH_C), cache_c_v.reshape(db, depth, past, KVH_C * DH_C),
                  cache_d_k.reshape(db, depth, past, H_D * DH_D), cache_d_v.reshape(db, depth, past, H_D * DH_D))
    y_p, y_s = x_prompt, x_sample
    collected = [[] for _ in range(8)]
    for l in range(depth):
        lq1, lk1, lq2, lk2 = diff_lambda[l].astype(F32)
        lam_init = 0.8 - 0.6 * math.exp(-0.3 * l)
        lam = (jnp.exp(jnp.sum(lq1 * lk1)) - jnp.exp(jnp.sum(lq2 * lk2)) + lam_init).reshape(1, 1)
        w_in_b = w_in[l].astype(BF16)
        lw = {
            'norm1_g': norm1_g[l][None, :], 'norm2_g': norm2_g[l][None, :], 'final_g': final_g[None, :],
            'w_qkv': w_in_b[:, :N_QKV], 'w_gate': w_in_b[:, N_QKV:],
            'lam': lam, 'lam_init': lam_init, 'diff_norm_g': diff_norm_g[l][None, :],
            'sink_c': sink_c[l][None, :].astype(F32),
            'ssm_mats': ssm_matrices(ssm_lam_re[l], ssm_lam_im[l], ssm_log_dt[l], ssm_b_re[l], ssm_b_im[l],
                                     ssm_c_re[l], ssm_c_im[l]),
            'ssm_d': ssm_d[l][None, :].astype(F32), 'w_glu': ssm_w_glu[l].astype(BF16),
            'w_branch': w_branch[l].astype(BF16), 'w_out': w_out[l].astype(BF16),
            'router_wt': router_wt, 'router_bt': router_bt,
            'w_e1': w_e1, 'w_e3': w_e3, 'w_e2': w_e2,
            'cos128': cos128, 'sin128': sin128,
            'bias_d': neighbourhood_bias(rpb_d[l], dn // GRID_W),
        }
        mods = jnp.split(mod_all[l], 6, axis=-1)
        mod_ctx = [mm[0:1][:, None, :] for mm in mods]
        mod_lat = [mm[1:n_cond][:, None, :] for mm in mods]
        final = l == depth - 1
        y_p, ctx_new = trunk_layer(y_p, mod_ctx, lw, l, None, final)
        y_s, _ = trunk_layer(y_s, mod_lat, lw, l, ctx_caches, final)
        for lst, t in zip(collected, ctx_new):
            lst.append(t)
    outs = [jnp.stack(lst, axis=1) for lst in collected]
    return (y_p, y_s, *outs)
```

```python
import functools
import math

import jax
import jax.numpy as jnp
import numpy as np
from jax import lax
from jax.experimental import pallas as pl
from jax.experimental.pallas import tpu as pltpu

F32 = jnp.float32
BF16 = jnp.bfloat16
HIGHEST = lax.Precision.HIGHEST

D_MODEL = 2048
DEPTH = 2
GRID_W = 64
EPS = 1e-6
ROPE_BASE = 10000.0
NEG_INF = -1e30
H_A, DH_A = 4, 64
DV_A = 2 * DH_A
W_B, CG_B, P_B = 512, 16, 64
G_B = W_B // CG_B
H_C, KVH_C, DH_C, WINDOW_C = 8, 2, 64, 128
H_D, DH_D, WIN_R, WIN_C = 8, 64, 8, 16
BRANCH_W, N_BRANCH = 512, 4
N_EXPERTS, N_GROUPS, TOP_K, D_EXPERT = 32, 8, 2, 512
EXPERTS_PER_GROUP = N_EXPERTS // N_GROUPS
N_QKV = 4352
OFF_A, OFF_U, OFF_C, OFF_D = 0, 1536, 2048, 2816
SSM_T = 16
LANES = 128
GROUPS_PER_TILE = LANES // CG_B
N_UTILES = W_B // LANES
MOE_ROWS = 256
VMEM_LIMIT = 56 * 1024 * 1024
LOG2E = math.log2(math.e)


def _cparams(sem):
    return pltpu.CompilerParams(dimension_semantics=sem, vmem_limit_bytes=VMEM_LIMIT)


def _dot(a, b):
    return jnp.dot(a, b, preferred_element_type=F32)


def _dot_nt(a, b):
    return lax.dot_general(a, b, (((1,), (1,)), ((), ())), preferred_element_type=F32)


def _adaln_kernel(c_ref, w_ref, b_ref, o_ref):
    a = c_ref[...]
    a = a * jax.nn.sigmoid(a)
    o_ref[...] = jnp.dot(a, w_ref[...], precision=HIGHEST, preferred_element_type=F32) + b_ref[...]


def adaln_all(cond, w_mod, b_mod, tn=1024):
    nl, d, n6 = w_mod.shape
    return pl.pallas_call(
        _adaln_kernel,
        grid=(nl, n6 // tn),
        in_specs=[pl.BlockSpec((cond.shape[0], d), lambda l, j: (0, 0)),
                  pl.BlockSpec((None, d, tn), lambda l, j: (l, 0, j)),
                  pl.BlockSpec((None, 1, tn), lambda l, j: (l, 0, j))],
        out_specs=pl.BlockSpec((None, cond.shape[0], tn), lambda l, j: (l, 0, j)),
        out_shape=jax.ShapeDtypeStruct((nl, cond.shape[0], n6), F32),
        compiler_params=_cparams(("arbitrary", "arbitrary")),
        name="adaln",
    )(cond, w_mod, b_mod)


def _norm_mod_kernel(x_ref, g_ref, sh_ref, sc_ref, o_ref):
    x = x_ref[...]
    r = lax.rsqrt(jnp.mean(x * x, axis=-1, keepdims=True) + EPS)
    o_ref[...] = ((x * r * g_ref[...]) * (1.0 + sc_ref[...]) + sh_ref[...]).astype(o_ref.dtype)


class ModRows:
    def __init__(self, mod, layer, row0, rows_per_mod):
        self.mod, self.layer, self.row0, self.rows_per_mod = mod, layer, row0, rows_per_mod

    def spec(self, chunk, tm):
        per = self.rows_per_mod // tm
        d = self.mod.shape[-1] // 6
        return pl.BlockSpec((None, None, 1, d), lambda i: (self.layer, self.row0 + i // per, 0, chunk))


def _layer_row_spec(layer, d):
    return pl.BlockSpec((None, 1, d), lambda *_: (layer, 0, 0))


def _const_spec(block_shape, index_map):
    return pl.BlockSpec(block_shape, index_map, pipeline_mode=pl.Buffered(1))


def norm_mod(x, g, mods, layer, tm=512):
    m, d = x.shape
    tm = min(tm, mods.rows_per_mod)
    return pl.pallas_call(
        _norm_mod_kernel,
        grid=(m // tm,),
        in_specs=[pl.BlockSpec((tm, d), lambda i: (i, 0)), _layer_row_spec(layer, d),
                  mods.spec(0, tm), mods.spec(1, tm)],
        out_specs=pl.BlockSpec((tm, d), lambda i: (i, 0)),
        out_shape=jax.ShapeDtypeStruct((m, d), BF16),
        compiler_params=_cparams(("parallel",)),
        name="norm_mod",
    )(x, g, mods.mod, mods.mod)


def _matmul_kernel(a_ref, w_ref, o_ref, *, act):
    acc = _dot(a_ref[...], w_ref[...])
    if act == "sigmoid":
        acc = jax.nn.sigmoid(acc)
    o_ref[...] = acc.astype(o_ref.dtype)


def matmul(a, w, layer, col0, n, out_dtype, act=None, tm=1024, tn=256, name="matmul"):
    m, k = a.shape
    tm = min(tm, m)
    off = col0 // tn
    assert col0 % tn == 0 and n % tn == 0
    return pl.pallas_call(
        functools.partial(_matmul_kernel, act=act),
        grid=(m // tm, n // tn),
        in_specs=[pl.BlockSpec((tm, k), lambda i, j: (i, 0)),
                  pl.BlockSpec((None, k, tn), lambda i, j: (layer, 0, off + j))],
        out_specs=pl.BlockSpec((tm, tn), lambda i, j: (i, j)),
        out_shape=jax.ShapeDtypeStruct((m, n), out_dtype),
        compiler_params=_cparams(("parallel", "arbitrary")),
        name=name,
    )(a, w)


def _softmax_parts(scores, extra=None):
    mx = scores[0].max(axis=-1, keepdims=True)
    for s in scores[1:]:
        mx = jnp.maximum(mx, s.max(axis=-1, keepdims=True))
    if extra is not None:
        mx = jnp.maximum(mx, extra)
    ps = [jnp.exp(s - mx) for s in scores]
    den = ps[0].sum(axis=-1, keepdims=True)
    for p in ps[1:]:
        den = den + p.sum(axis=-1, keepdims=True)
    if extra is not None:
        den = den + jnp.exp(extra - mx)
    return ps, den


def _head_rmsnorm(o, g, mult):
    r = lax.rsqrt(jnp.mean(o * o, axis=-1, keepdims=True) + EPS)
    return o * r * g * mult


CACHE_COLS = ((512, 1024), (1024, 1536), (OFF_C + 512, OFF_C + 640), (OFF_C + 640, OFF_C + 768),
              (OFF_D + 512, OFF_D + 1024), (OFF_D + 1024, OFF_D + 1536))


def _ctx_mixers_kernel(s_ref, lam_ref, ng_ref, sink_ref, *rest, lam_init):
    n_cache = len(CACHE_COLS)
    oa_ref, oc_ref, od_ref = rest[n_cache:n_cache + 3]
    for (c0, c1), dst in zip(CACHE_COLS, rest[n_cache + 3:]):
        dst[...] = s_ref[:, c0:c1]
    lam = lam_ref[...]
    for h in range(H_A):
        v = s_ref[:, 1024 + h * DV_A:1024 + (h + 1) * DV_A].astype(BF16)
        outs = []
        for mi in range(2):
            c0 = OFF_A + (h * 2 + mi) * DH_A
            q = (s_ref[:, c0:c0 + DH_A] * (DH_A ** -0.5)).astype(BF16)
            k = s_ref[:, 512 + c0:512 + c0 + DH_A].astype(BF16)
            (p,), den = _softmax_parts([_dot_nt(q, k)])
            outs.append(_dot(p.astype(BF16), v) / den)
        o = _head_rmsnorm(outs[0] - lam * outs[1], ng_ref[...], 1.0 - lam_init)
        oa_ref[:, h * DV_A:(h + 1) * DV_A] = o.astype(oa_ref.dtype)
    grp = H_C // KVH_C
    for h in range(H_C):
        n = h // grp
        q = (s_ref[:, OFF_C + h * DH_C:OFF_C + (h + 1) * DH_C] * (DH_C ** -0.5)).astype(BF16)
        k = s_ref[:, OFF_C + 512 + n * DH_C:OFF_C + 512 + (n + 1) * DH_C].astype(BF16)
        v = s_ref[:, OFF_C + 640 + n * DH_C:OFF_C + 640 + (n + 1) * DH_C].astype(BF16)
        sink = jnp.broadcast_to(sink_ref[:, h:h + 1], (q.shape[0], 1))
        (p,), den = _softmax_parts([_dot_nt(q, k)], extra=sink)
        o = _dot(p.astype(BF16), v) / den
        oc_ref[:, h * DH_C:(h + 1) * DH_C] = o.astype(oc_ref.dtype)
    for h in range(H_D):
        q = (s_ref[:, OFF_D + h * DH_D:OFF_D + (h + 1) * DH_D] * (DH_D ** -0.5)).astype(BF16)
        k = s_ref[:, OFF_D + 512 + h * DH_D:OFF_D + 512 + (h + 1) * DH_D].astype(BF16)
        v = s_ref[:, OFF_D + 1024 + h * DH_D:OFF_D + 1024 + (h + 1) * DH_D].astype(BF16)
        (p,), den = _softmax_parts([_dot_nt(q, k)])
        o = _dot(p.astype(BF16), v) / den
        od_ref[:, h * DH_D:(h + 1) * DH_D] = o.astype(od_ref.dtype)


def ctx_mixers(slab, lam, lam_init, norm_g, sink, layer, caches):
    b, s, _ = slab.shape
    o_spec = pl.BlockSpec((None, s, BRANCH_W), lambda i: (i, 0, 0))
    o_shape = jax.ShapeDtypeStruct((b, s, BRANCH_W), BF16)
    n_cache = len(caches)
    cache_specs = [pl.BlockSpec((None, None, s, c.shape[-1]), lambda i: (i, layer, 0, 0)) for c in caches]
    outs = pl.pallas_call(
        functools.partial(_ctx_mixers_kernel, lam_init=lam_init),
        grid=(b,),
        in_specs=[pl.BlockSpec((None, s, N_QKV), lambda i: (i, 0, 0)),
                  _layer_row_spec(layer, 1), _layer_row_spec(layer, DV_A), _layer_row_spec(layer, H_C)]
        + [pl.BlockSpec(memory_space=pl.ANY)] * n_cache,
        out_specs=[o_spec, o_spec, o_spec] + cache_specs,
        out_shape=[o_shape, o_shape, o_shape] + [jax.ShapeDtypeStruct(c.shape, c.dtype) for c in caches],
        input_output_aliases={4 + k: 3 + k for k in range(n_cache)},
        compiler_params=_cparams(("parallel",)),
        name="ctx_mixers",
    )(slab, lam, norm_g, sink, *caches)
    return outs[0], outs[1], outs[2], tuple(outs[3:])


def _rope128(x, cos, sin):
    lane = lax.broadcasted_iota(jnp.int32, x.shape, 1)
    other = jnp.where((lane % 64) < 32, pltpu.roll(x, 96, axis=1), pltpu.roll(x, 32, axis=1))
    return x * cos + other * sin


def _lat_prep_kernel(s_ref, cos_ref, sin_ref, a_ref, c_ref, d_ref):
    cos, sin = cos_ref[...], sin_ref[...]

    def put(dst, dst0, src0, rope, scale):
        x = s_ref[:, src0:src0 + 128]
        if rope:
            x = _rope128(x, cos, sin)
        if scale != 1.0:
            x = x * scale
        dst[:, dst0:dst0 + 128] = x.astype(dst.dtype)

    for j in range(12):
        put(a_ref, j * 128, OFF_A + j * 128, j < 8, DH_A ** -0.5 * LOG2E if j < 4 else 1.0)
    for j in range(6):
        if j < 4:
            put(c_ref, j * 128, OFF_C + j * 128, True, DH_C ** -0.5)
        elif j == 4:
            put(c_ref, j * 128, OFF_C + j * 128, True, 1.0)
        else:
            put(c_ref, j * 128, OFF_C + j * 128, False, 1.0)
    for j in range(12):
        put(d_ref, j * 128, OFF_D + j * 128, False, DH_D ** -0.5 if j < 4 else 1.0)


def lat_prep(slab, cos128, sin128, tm=512):
    b, n, _ = slab.shape
    tm = min(tm, n)
    widths = (1536, 768, 1536)
    return pl.pallas_call(
        _lat_prep_kernel,
        grid=(b, n // tm),
        in_specs=[pl.BlockSpec((None, tm, N_QKV), lambda i, j: (i, j, 0)),
                  pl.BlockSpec((tm, 128), lambda i, j: (j, 0)),
                  pl.BlockSpec((tm, 128), lambda i, j: (j, 0))],
        out_specs=[pl.BlockSpec((None, tm, w), lambda i, j: (i, j, 0)) for w in widths],
        out_shape=[jax.ShapeDtypeStruct((b, n, w), BF16) for w in widths],
        compiler_params=_cparams(("parallel", "parallel")),
        name="lat_prep",
    )(slab, cos128, sin128)


def _lat_a_kernel(q_ref, k_ref, v_ref, kc_ref, vc_ref, lam_ref, ng_ref, o_ref, *, tk, lam_init):
    n = k_ref.shape[0]
    tq = q_ref.shape[0]
    q = q_ref[...]
    qs = (q[:, :DH_A], q[:, DH_A:])

    def update(state, kblk, vblk):
        new = []
        for mi in range(2):
            m_old, l_old, acc = state[mi]
            s = _dot_nt(qs[mi], kblk[:, mi * DH_A:(mi + 1) * DH_A])
            m_new = jnp.maximum(m_old, s.max(axis=-1, keepdims=True))
            alpha = jnp.exp2(m_old - m_new)
            p = jnp.exp2(s - m_new)
            l_new = alpha * l_old + p.sum(axis=-1, keepdims=True)
            acc = alpha * acc + _dot(p.astype(BF16), vblk)
            new.append((m_new, l_new, acc))
        return tuple(new)

    init = tuple((jnp.full((tq, 1), -jnp.inf, F32), jnp.zeros((tq, 1), F32), jnp.zeros((tq, DV_A), F32))
                 for _ in range(2))
    state = update(init, kc_ref[...].astype(BF16), vc_ref[...].astype(BF16))

    def body(j, st):
        r0 = pl.multiple_of(j * tk, tk)
        return update(st, k_ref[pl.ds(r0, tk), :], v_ref[pl.ds(r0, tk), :])

    state = lax.fori_loop(0, n // tk, body, state)
    (_, l1, a1), (_, l2, a2) = state
    o = a1 / l1 - lam_ref[...] * (a2 / l2)
    o_ref[...] = _head_rmsnorm(o, ng_ref[...], 1.0 - lam_init).astype(o_ref.dtype)


def lat_attn_a(a_qkv, ctx_k, ctx_v, layer, lam, lam_init, norm_g, tq=1024, tk=512):
    b, n, _ = a_qkv.shape
    past = ctx_k.shape[2]
    tq, tk = min(tq, n), min(tk, n)
    return pl.pallas_call(
        functools.partial(_lat_a_kernel, tk=tk, lam_init=lam_init),
        grid=(b, H_A, n // tq),
        in_specs=[pl.BlockSpec((None, tq, 128), lambda i, h, j: (i, j, h)),
                  pl.BlockSpec((None, n, 128), lambda i, h, j: (i, 0, H_A + h)),
                  pl.BlockSpec((None, n, 128), lambda i, h, j: (i, 0, 2 * H_A + h)),
                  pl.BlockSpec((None, None, past, 128), lambda i, h, j: (i, layer, 0, h)),
                  pl.BlockSpec((None, None, past, 128), lambda i, h, j: (i, layer, 0, h)),
                  _layer_row_spec(layer, 1), _layer_row_spec(layer, DV_A)],
        out_specs=pl.BlockSpec((None, tq, DV_A), lambda i, h, j: (i, j, h)),
        out_shape=jax.ShapeDtypeStruct((b, n, BRANCH_W), BF16),
        compiler_params=_cparams(("parallel", "parallel", "arbitrary")),
        name="lat_attn_a",
    )(a_qkv, a_qkv, a_qkv, ctx_k, ctx_v, lam, norm_g)


def _lat_c_kernel(q_ref, kv_ref, kc_ref, vc_ref, sink_ref, o_ref):
    blk = q_ref.shape[0]
    n_tok = kv_ref.shape[0]
    span = 3 * blk
    i = pl.program_id(1)
    start = pl.multiple_of(jnp.clip((i - 1) * blk, 0, n_tok - span), blk)
    kv = kv_ref[pl.ds(start, span), :]
    kc = kc_ref[...].astype(BF16)
    vc = vc_ref[...].astype(BF16)
    grp = H_C // KVH_C
    qpos = i * blk + lax.broadcasted_iota(jnp.int32, (grp * blk, span), 0) % blk
    kpos = start + lax.broadcasted_iota(jnp.int32, (grp * blk, span), 1)
    band = jnp.abs(kpos - qpos) <= WINDOW_C
    for n in range(KVH_C):
        heads = range(n * grp, (n + 1) * grp)
        q = jnp.concatenate([q_ref[:, h * DH_C:(h + 1) * DH_C] for h in heads], axis=0)
        sink = jnp.concatenate([jnp.broadcast_to(sink_ref[:, h:h + 1], (blk, 1)) for h in heads], axis=0)
        s_loc = jnp.where(band, _dot_nt(q, kv[:, n * DH_C:(n + 1) * DH_C]), NEG_INF)
        s_ctx = _dot_nt(q, kc[:, n * DH_C:(n + 1) * DH_C])
        (p_loc, p_ctx), den = _softmax_parts([s_loc, s_ctx], extra=sink)
        v_loc = kv[:, KVH_C * DH_C + n * DH_C:KVH_C * DH_C + (n + 1) * DH_C]
        o = (_dot(p_loc.astype(BF16), v_loc) + _dot(p_ctx.astype(BF16), vc[:, n * DH_C:(n + 1) * DH_C])) / den
        for g, h in enumerate(heads):
            o_ref[:, h * DH_C:(h + 1) * DH_C] = o[g * blk:(g + 1) * blk].astype(o_ref.dtype)


def lat_attn_c(c_qkv, ctx_k, ctx_v, layer, sink):
    b, n, _ = c_qkv.shape
    past = ctx_k.shape[2]
    blk = WINDOW_C
    assert n % blk == 0 and n >= 3 * blk
    return pl.pallas_call(
        _lat_c_kernel,
        grid=(b, n // blk),
        in_specs=[pl.BlockSpec((None, blk, 512), lambda i, j: (i, j, 0)),
                  pl.BlockSpec((None, n, 256), lambda i, j: (i, 0, 2)),
                  pl.BlockSpec((None, None, past, 128), lambda i, j: (i, layer, 0, 0)),
                  pl.BlockSpec((None, None, past, 128), lambda i, j: (i, layer, 0, 0)),
                  _layer_row_spec(layer, H_C)],
        out_specs=pl.BlockSpec((None, blk, BRANCH_W), lambda i, j: (i, j, 0)),
        out_shape=jax.ShapeDtypeStruct((b, n, BRANCH_W), BF16),
        compiler_params=_cparams(("parallel", "arbitrary")),
        name="lat_attn_c",
    )(c_qkv, c_qkv, ctx_k, ctx_v, sink)


D_ROWS = 4


def _nbr_geometry(rows):
    wr = min(WIN_R, rows)
    rblk = min(D_ROWS, rows)
    span = min(rblk + wr - 1, rows)
    bases, cls, patterns, seen = [], [], [], {}
    for rb in range(rows // rblk):
        r = rb * rblk + np.arange(rblk)
        r0 = np.clip(r - wr // 2, 0, rows - wr)
        base = int(np.clip(r0.min(), 0, rows - span))
        kr = base + np.arange(span)
        valid = (kr[None, :] >= r0[:, None]) & (kr[None, :] < r0[:, None] + wr)
        assert valid.sum(axis=1).min() == wr
        dr_idx = np.where(valid, kr[None, :] - r[:, None] + WIN_R - 1, 0)
        key = dr_idx.tobytes() + valid.tobytes()
        if key not in seen:
            seen[key] = len(patterns)
            patterns.append((dr_idx, valid))
        bases.append(base)
        cls.append(seen[key])
    return wr, rblk, span, np.asarray(bases, np.int32), np.asarray(cls, np.int32), patterns


def _lat_d_kernel(base_ref, cls_ref, q_ref, k_ref, v_ref, kc_ref, vc_ref, bias_ref, o_ref, *, span):
    t0 = pl.multiple_of(base_ref[pl.program_id(1)] * GRID_W, GRID_W)
    kb = k_ref[pl.ds(t0, span * GRID_W), :]
    vb = v_ref[pl.ds(t0, span * GRID_W), :]
    kc = kc_ref[...].astype(BF16)
    vc = vc_ref[...].astype(BF16)
    for h in range(H_D):
        sl = slice(h * DH_D, (h + 1) * DH_D)
        q = q_ref[:, sl]
        s_loc = _dot_nt(q, kb[:, sl]) + bias_ref[h]
        s_ctx = _dot_nt(q, kc[:, sl])
        (p_loc, p_ctx), den = _softmax_parts([s_loc, s_ctx])
        o = (_dot(p_loc.astype(BF16), vb[:, sl]) + _dot(p_ctx.astype(BF16), vc[:, sl])) / den
        o_ref[:, sl] = o.astype(o_ref.dtype)


def lat_attn_d(d_qkv, ctx_k, ctx_v, layer, bias_tab):
    b, n, _ = d_qkv.shape
    past = ctx_k.shape[2]
    rows = n // GRID_W
    _, rblk, span, base, cls, _ = _nbr_geometry(rows)
    nq = rblk * GRID_W
    grid_spec = pltpu.PrefetchScalarGridSpec(
        num_scalar_prefetch=2,
        grid=(b, rows // rblk),
        in_specs=[pl.BlockSpec((None, nq, 512), lambda i, r, bs, cl: (i, r, 0)),
                  pl.BlockSpec((None, n, 512), lambda i, r, bs, cl: (i, 0, 1)),
                  pl.BlockSpec((None, n, 512), lambda i, r, bs, cl: (i, 0, 2)),
                  pl.BlockSpec((None, None, past, 512), lambda i, r, bs, cl: (i, layer, 0, 0)),
                  pl.BlockSpec((None, None, past, 512), lambda i, r, bs, cl: (i, layer, 0, 0)),
                  pl.BlockSpec((None, None, H_D, nq, span * GRID_W), lambda i, r, bs, cl: (layer, cl[r], 0, 0, 0))],
        out_specs=pl.BlockSpec((None, nq, BRANCH_W), lambda i, r, bs, cl: (i, r, 0)),
    )
    return pl.pallas_call(
        functools.partial(_lat_d_kernel, span=span),
        grid_spec=grid_spec,
        out_shape=jax.ShapeDtypeStruct((b, n, BRANCH_W), BF16),
        compiler_params=_cparams(("parallel", "arbitrary")),
        name="lat_attn_d",
    )(jnp.asarray(base), jnp.asarray(cls), d_qkv, d_qkv, d_qkv, ctx_k, ctx_v, bias_tab)


def neighbourhood_bias(rpb, rows):
    _, rblk, span, _, _, patterns = _nbr_geometry(rows)
    cols = np.arange(GRID_W)
    cstart = np.clip(cols - WIN_C // 2, 0, GRID_W - WIN_C)
    inside_c = (cols[None, :] >= cstart[:, None]) & (cols[None, :] < cstart[:, None] + WIN_C)
    dc_idx = np.clip(cols[None, :] - cols[:, None] + WIN_C - 1, 0, 2 * WIN_C - 2)
    dr_idx = np.stack([p[0] for p in patterns])
    valid = np.stack([p[1] for p in patterns])
    ok = valid[:, :, :, None, None] & inside_c[None, None, None]
    flat = (dr_idx[:, :, :, None, None] * (2 * WIN_C - 1) + dc_idx[None, None, None]).reshape(-1)
    nl = rpb.shape[0]
    tab = jnp.take(rpb.astype(F32).reshape(nl, H_D, -1), flat, axis=2).reshape((nl, H_D) + ok.shape)
    tab = jnp.where(ok[None, None], tab, NEG_INF)
    tab = jnp.transpose(tab, (0, 2, 1, 3, 5, 4, 6))
    return tab.reshape(nl, len(patterns), H_D, rblk * GRID_W, span * GRID_W)


def ssm_matrices(lam_re, lam_im, log_dt, b_re, b_im, c_re, c_im):
    t = SSM_T
    f32 = lambda a: a.astype(F32)
    nl = lam_re.shape[0]
    dt = jnp.exp(f32(log_dt))[..., None]
    lr, li = f32(lam_re), f32(lam_im)
    zr, zi = lr * dt, li * dt
    taus = jnp.arange(t + 1, dtype=F32).reshape(t + 1, 1, 1, 1, 1)
    mag = jnp.exp(zr[None] * taus)
    ar, ai = mag * jnp.cos(zi[None] * taus), mag * jnp.sin(zi[None] * taus)
    den = lr * lr + li * li
    nr, ni = ar[1] - 1.0, ai[1]
    qr, qi = (nr * lr + ni * li) / den, (ni * lr - nr * li) / den
    br, bi = f32(b_re), f32(b_im)
    bbr = qr[..., None] * br - qi[..., None] * bi
    bbi = qr[..., None] * bi + qi[..., None] * br
    cr, ci = f32(c_re), f32(c_im)
    pr = ar[:t, ..., None] * bbr[None] - ai[:t, ..., None] * bbi[None]
    pi = ar[:t, ..., None] * bbi[None] + ai[:t, ..., None] * bbr[None]
    kk = (jnp.einsum('ldgep,tldgpc->tldgec', cr, pr, precision=HIGHEST)
          - jnp.einsum('ldgep,tldgpc->tldgec', ci, pi, precision=HIGHEST))
    jj = np.arange(t)[:, None]
    tt = np.arange(t)[None, :]
    m = None
    es, fs, decs = [], [], []
    for d in range(2):
        lag = (tt - jj) if d == 0 else (jj - tt)
        md = kk[np.clip(lag, 0, t - 1), :, d]
        md = jnp.where((lag >= 0)[:, :, None, None, None, None], md, 0.0)
        m = md if m is None else m + md
        epow = (t - 1 - np.arange(t)) if d == 0 else np.arange(t)
        for p in (pr, pi):
            es.append(jnp.transpose(p[epow, :, d], (1, 2, 0, 4, 3)).reshape(nl, G_B, t * CG_B, P_B))
        fpow = (np.arange(t) + 1) if d == 0 else (t - np.arange(t))
        far, fai = ar[fpow, :, d][:, :, :, None, :], ai[fpow, :, d][:, :, :, None, :]
        crd, cid = cr[None, :, d], ci[None, :, d]
        for fm in (crd * far - cid * fai, -(crd * fai + cid * far)):
            fs.append(jnp.transpose(fm, (1, 2, 4, 0, 3)).reshape(nl, G_B, P_B, t * CG_B))
        decs += [ar[t, :, d], ai[t, :, d]]
    m = jnp.transpose(m, (2, 3, 0, 5, 1, 4)).reshape(nl, G_B, t * CG_B, t * CG_B)
    pad = LANES - P_B
    e = jnp.pad(jnp.stack(es, axis=1), ((0, 0),) * 4 + ((0, pad),))
    f = jnp.pad(jnp.stack(fs, axis=1), ((0, 0),) * 3 + ((0, pad), (0, 0)))
    dec = jnp.pad(jnp.stack(decs, axis=2)[:, :, :, None, :], ((0, 0),) * 4 + ((0, pad),))
    return m.astype(BF16), e.astype(BF16), f.astype(BF16), dec


def _ssm_fold_kernel(*refs):
    u_refs, o_ref, scr = refs[:N_UTILES], refs[N_UTILES], refs[N_UTILES + 1]
    n = o_ref.shape[1]
    for a, u_ref in enumerate(u_refs):
        for j in range(SSM_T):
            x = u_ref[pl.ds(j, n, stride=SSM_T), :]
            for i in range(GROUPS_PER_TILE):
                scr[a * GROUPS_PER_TILE + i, :, j * CG_B:(j + 1) * CG_B] = x[:, i * CG_B:(i + 1) * CG_B]
    o_ref[...] = scr[...].astype(o_ref.dtype)


def _ssm_unfold_kernel(*refs):
    y_ref, o_refs, scr = refs[0], refs[1:1 + N_UTILES], refs[1 + N_UTILES]
    n = y_ref.shape[1]
    for a, o_ref in enumerate(o_refs):
        for j in range(SSM_T):
            for i in range(GROUPS_PER_TILE):
                scr[j, :, i * CG_B:(i + 1) * CG_B] = y_ref[a * GROUPS_PER_TILE + i, :, j * CG_B:(j + 1) * CG_B]
        for j in range(SSM_T):
            o_ref[pl.ds(j, n, stride=SSM_T), :] = scr[j]


def _ssm_kernel(u_ref, m_ref, e_ref, f_ref, dec_ref, s0_ref, y_ref, fin_ref, x_s, c_s):
    nb, nc, tc = u_ref.shape
    u = u_ref[...].reshape(nb * nc, tc)
    for i in range(4):
        x_s[i] = _dot(u, e_ref[i])
    dfr, dfi, dbr, dbi = (dec_ref[i] for i in range(4))

    def body(k, carry):
        fr, fi, br, bi = carry
        kf = pl.ds(k, nb, stride=nc)
        kb = pl.ds(nc - 1 - k, nb, stride=nc)
        c_s[0, kf, :] = fr
        c_s[1, kf, :] = fi
        c_s[2, kb, :] = br
        c_s[3, kb, :] = bi
        return (dfr * fr - dfi * fi + x_s[0, kf, :], dfr * fi + dfi * fr + x_s[1, kf, :],
                dbr * br - dbi * bi + x_s[2, kb, :], dbr * bi + dbi * br + x_s[3, kb, :])

    fin = lax.fori_loop(0, nc, body, tuple(s0_ref[i] for i in range(4)))
    for i in range(4):
        fin_ref[i] = fin[i]
    y = _dot(u, m_ref[...])
    for i in range(4):
        y = y + _dot(c_s[i].astype(BF16), f_ref[i])
    y_ref[...] = y.reshape(nb, nc, tc)


def ssm_scan(slab3, mats, layer, s0_re, s0_im, ts=1024):
    nb, s, _ = slab3.shape
    t, tc = SSM_T, SSM_T * CG_B
    nc = s // t
    ts = min(ts, s)
    n = ts // t
    m_all, e_all, f_all, dec_all = mats
    u_tile0 = OFF_U // LANES
    u_fold = pl.pallas_call(
        _ssm_fold_kernel,
        grid=(nb, s // ts),
        in_specs=[pl.BlockSpec((None, ts, LANES), functools.partial(lambda a, b, i: (b, i, u_tile0 + a), a))
                  for a in range(N_UTILES)],
        out_specs=pl.BlockSpec((G_B, None, n, tc), lambda b, i: (0, b, i, 0)),
        out_shape=jax.ShapeDtypeStruct((G_B, nb, nc, tc), BF16),
        scratch_shapes=[pltpu.VMEM((G_B, n, tc), F32)],
        compiler_params=_cparams(("parallel", "parallel")),
        name="ssm_fold",
    )(*([slab3] * N_UTILES))
    s0 = jnp.stack([s0_re[:, 0], s0_im[:, 0], s0_re[:, 1], s0_im[:, 1]]).astype(F32)
    s0 = jnp.pad(jnp.transpose(s0, (2, 0, 1, 3)), ((0, 0), (0, 0), (0, 0), (0, LANES - P_B)))
    y_fold, fin = pl.pallas_call(
        _ssm_kernel,
        grid=(G_B,),
        in_specs=[pl.BlockSpec((None, nb, nc, tc), lambda g: (g, 0, 0, 0)),
                  pl.BlockSpec((None, None, tc, tc), lambda g: (layer, g, 0, 0)),
                  pl.BlockSpec((None, 4, None, tc, LANES), lambda g: (layer, 0, g, 0, 0)),
                  pl.BlockSpec((None, 4, None, LANES, tc), lambda g: (layer, 0, g, 0, 0)),
                  pl.BlockSpec((None, None, 4, 1, LANES), lambda g: (layer, g, 0, 0, 0)),
                  pl.BlockSpec((None, 4, nb, LANES), lambda g: (g, 0, 0, 0))],
        out_specs=[pl.BlockSpec((None, nb, nc, tc), lambda g: (g, 0, 0, 0)),
                   pl.BlockSpec((None, 4, nb, LANES), lambda g: (g, 0, 0, 0))],
        out_shape=[jax.ShapeDtypeStruct((G_B, nb, nc, tc), F32), jax.ShapeDtypeStruct((G_B, 4, nb, LANES), F32)],
        scratch_shapes=[pltpu.VMEM((4, nb * nc, LANES), F32), pltpu.VMEM((4, nb * nc, LANES), F32)],
        compiler_params=_cparams(("parallel",)),
        name="ssm_scan",
    )(u_fold, m_all, e_all, f_all, dec_all, s0)
    y_tiles = pl.pallas_call(
        _ssm_unfold_kernel,
        grid=(nb, s // ts),
        in_specs=[pl.BlockSpec((G_B, None, n, tc), lambda b, i: (0, b, i, 0))],
        out_specs=[pl.BlockSpec((None, ts, LANES), lambda b, i: (b, i, 0))] * N_UTILES,
        out_shape=[jax.ShapeDtypeStruct((nb, s, LANES), F32)] * N_UTILES,
        scratch_shapes=[pltpu.VMEM((SSM_T, n, LANES), F32)],
        compiler_params=_cparams(("parallel", "parallel")),
        name="ssm_unfold",
    )(y_fold)
    fin = jnp.transpose(fin[..., :P_B], (2, 1, 0, 3))
    return y_tiles, fin[:, 0::2], fin[:, 1::2]


def _merge_kernel(oa_ref, oc_ref, od_ref, u_ref, *rest):
    ys_refs = rest[:N_UTILES]
    gate_ref, dsk_ref, wglu_ref, wb_ref, o_ref = rest[N_UTILES:]
    ys = jnp.concatenate([r[...] for r in ys_refs], axis=1)
    y = jax.nn.gelu(u_ref[...] * dsk_ref[...] + ys)
    z = _dot(y.astype(BF16), wglu_ref[...])
    ob = (z[:, :W_B] * jax.nn.sigmoid(z[:, W_B:])).astype(BF16)
    acc = None
    for i, o in enumerate((oa_ref[...], ob, oc_ref[...], od_ref[...])):
        term = gate_ref[:, i * D_MODEL:(i + 1) * D_MODEL].astype(F32) * _dot(o, wb_ref[i])
        acc = term if acc is None else acc + term
    o_ref[...] = acc.astype(o_ref.dtype)


def branch_merge(o_a, o_c, o_d, slab, y_ssm, gates, ssm_d, w_glu, w_branch, layer, tm=256):
    m = o_a.shape[0]
    tm = min(tm, m)
    row = lambda w: pl.BlockSpec((tm, w), lambda i: (i, 0))
    return pl.pallas_call(
        _merge_kernel,
        grid=(m // tm,),
        in_specs=[row(BRANCH_W), row(BRANCH_W), row(BRANCH_W),
                  pl.BlockSpec((tm, W_B), lambda i: (i, OFF_U // W_B))]
        + [row(LANES)] * N_UTILES
        + [row(N_BRANCH * D_MODEL), _layer_row_spec(layer, W_B),
           _const_spec((None, W_B, 2 * W_B), lambda i: (layer, 0, 0)),
           _const_spec((None, N_BRANCH, BRANCH_W, D_MODEL), lambda i: (layer, 0, 0, 0))],
        out_specs=row(D_MODEL),
        out_shape=jax.ShapeDtypeStruct((m, D_MODEL), BF16),
        compiler_params=_cparams(("parallel",)),
        name="branch_merge",
    )(o_a, o_c, o_d, slab, *y_ssm, gates, ssm_d, w_glu, w_branch)


def _route_select(sc, bi):
    a = [bi[j * N_GROUPS:(j + 1) * N_GROUPS] for j in range(EXPERTS_PER_GROUP)]
    s = [sc[j * N_GROUPS:(j + 1) * N_GROUPS] for j in range(EXPERTS_PER_GROUP)]

    def first_eq(v, m):
        return jnp.where(v[0] == m, 0, jnp.where(v[1] == m, 1, jnp.where(v[2] == m, 2, 3)))

    def by_slot(v, i):
        return jnp.where(i == 0, v[0], jnp.where(i == 1, v[1], jnp.where(i == 2, v[2], v[3])))

    m1 = jnp.maximum(jnp.maximum(a[0], a[1]), jnp.maximum(a[2], a[3]))
    i1 = first_eq(a, m1)
    b = [jnp.where(i1 == j, -jnp.inf, a[j]) for j in range(EXPERTS_PER_GROUP)]
    m2 = jnp.maximum(jnp.maximum(b[0], b[1]), jnp.maximum(b[2], b[3]))
    i2 = first_eq(b, m2)
    gs = m1 + m2
    grow = lax.broadcasted_iota(jnp.int32, gs.shape, 0)
    gsel = jnp.min(jnp.where(gs == jnp.max(gs, axis=0, keepdims=True), grow, N_GROUPS), axis=0, keepdims=True)
    oh = grow == gsel
    pick_i = lambda v: jnp.sum(jnp.where(oh, v, 0), axis=0, keepdims=True)
    pick_f = lambda v: jnp.sum(jnp.where(oh, v, 0.0), axis=0, keepdims=True)
    return gsel, pick_i(i1), pick_i(i2), pick_f(by_slot(s, i1)), pick_f(by_slot(s, i2))


def _out_kernel(x_ref, mg_ref, w_ref, g1_ref, n2_ref, sh_ref, sc_ref, rw_ref, rb_ref,
                xo_ref, h_ref, ri_ref, rwt_ref, cnt_ref, run_s):
    @pl.when(pl.program_id(0) == 0)
    def _():
        run_s[...] = jnp.zeros_like(run_s)

    x = x_ref[...] + g1_ref[...] * _dot(mg_ref[...], w_ref[...])
    xo_ref[...] = x
    r = lax.rsqrt(jnp.mean(x * x, axis=-1, keepdims=True) + EPS)
    h = (x * r * n2_ref[...]) * (1.0 + sc_ref[...]) + sh_ref[...]
    h_ref[...] = h.astype(h_ref.dtype)
    tm = h.shape[0]
    logits = lax.dot_general(rw_ref[...], h, (((1,), (1,)), ((), ())), precision=HIGHEST,
                             preferred_element_type=F32)
    sc = jax.nn.sigmoid(logits)
    gsel, l1, l2, s1, s2 = _route_select(sc, sc + rb_ref[...])
    den = s1 + s2
    rowid = lax.broadcasted_iota(jnp.int32, (N_EXPERTS, tm), 0)
    oh1 = rowid == l1 * N_GROUPS + gsel
    oh2 = rowid == l2 * N_GROUPS + gsel
    ohf = jnp.where(jnp.logical_or(oh1, oh2), 1.0, 0.0)
    before = lax.broadcasted_iota(jnp.int32, (tm, tm), 0) < lax.broadcasted_iota(jnp.int32, (tm, tm), 1)
    prefix = _dot(ohf.astype(BF16), jnp.where(before, 1.0, 0.0).astype(BF16)) + run_s[:, 0:1]
    rank1 = jnp.sum(jnp.where(oh1, prefix, 0.0), axis=0, keepdims=True)
    rank2 = jnp.sum(jnp.where(oh2, prefix, 0.0), axis=0, keepdims=True)
    run_s[...] = run_s[...] + jnp.sum(ohf, axis=1, keepdims=True)
    cnt_ref[...] = run_s[...]
    e1 = gsel * EXPERTS_PER_GROUP + l1
    e2 = gsel * EXPERTS_PER_GROUP + l2
    ri_ref[...] = jnp.concatenate([e1, e2, rank1.astype(jnp.int32), rank2.astype(jnp.int32),
                                   jnp.zeros((4, tm), jnp.int32)], axis=0)
    rwt_ref[...] = jnp.concatenate([s1 / den, s2 / den, jnp.zeros((6, tm), F32)], axis=0)


def out_proj(x, merged, w_out, norm2_g, mods, layer, router_wt, router_bt, tm=512):
    m, d = x.shape
    tm = min(tm, mods.rows_per_mod)
    row = lambda w: pl.BlockSpec((tm, w), lambda i: (i, 0))
    col = pl.BlockSpec((8, tm), lambda i: (0, i))
    return pl.pallas_call(
        _out_kernel,
        grid=(m // tm,),
        in_specs=[row(d), row(d), _const_spec((None, d, d), lambda i: (layer, 0, 0)), mods.spec(2, tm),
                  _layer_row_spec(layer, d), mods.spec(3, tm), mods.spec(4, tm),
                  _const_spec((N_EXPERTS, d), lambda i: (0, 0)),
                  _const_spec((N_EXPERTS, 1), lambda i: (0, 0))],
        out_specs=[row(d), row(d), col, col, pl.BlockSpec((N_EXPERTS, 128), lambda i: (0, 0))],
        out_shape=[jax.ShapeDtypeStruct((m, d), F32), jax.ShapeDtypeStruct((m, d), BF16),
                   jax.ShapeDtypeStruct((8, m), jnp.int32), jax.ShapeDtypeStruct((8, m), F32),
                   jax.ShapeDtypeStruct((N_EXPERTS, 128), F32)],
        scratch_shapes=[pltpu.VMEM((N_EXPERTS, 128), F32)],
        compiler_params=_cparams(("arbitrary",)),
        name="out_proj",
    )(x, merged, w_out, mods.mod, norm2_g, mods.mod, mods.mod, router_wt, router_bt)


def _experts_kernel(be_ref, nu_ref, x_ref, w1_ref, w3_ref, w2_ref, o_ref, w1_s, w3_s, w2_s):
    i = pl.program_id(0)
    changed = jnp.logical_or(i == 0, be_ref[i] != be_ref[jnp.maximum(i - 1, 0)])

    @pl.when(jnp.logical_and(changed, i < nu_ref[0]))
    def _():
        w1_s[...] = w1_ref[...].astype(BF16)
        w3_s[...] = w3_ref[...].astype(BF16)
        w2_s[...] = w2_ref[...].astype(BF16)

    @pl.when(i < nu_ref[0])
    def _():
        x = x_ref[...]
        a = _dot(x, w1_s[...])
        hid = (a * jax.nn.sigmoid(a)) * _dot(x, w3_s[...])
        o_ref[...] = _dot(hid.astype(BF16), w2_s[...]).astype(o_ref.dtype)

    @pl.when(i >= nu_ref[0])
    def _():
        o_ref[...] = jnp.zeros_like(o_ref)


def experts(buf, block_e, n_used, w1, w3, w2, layer):
    rows, d = buf.shape
    f = w1.shape[-1]
    nblk = rows // MOE_ROWS
    grid_spec = pltpu.PrefetchScalarGridSpec(
        num_scalar_prefetch=2,
        grid=(nblk,),
        in_specs=[pl.BlockSpec((MOE_ROWS, d), lambda i, be, nu: (i, 0)),
                  pl.BlockSpec((None, None, d, f), lambda i, be, nu: (layer, be[i], 0, 0)),
                  pl.BlockSpec((None, None, d, f), lambda i, be, nu: (layer, be[i], 0, 0)),
                  pl.BlockSpec((None, None, f, d), lambda i, be, nu: (layer, be[i], 0, 0))],
        out_specs=pl.BlockSpec((MOE_ROWS, d), lambda i, be, nu: (i, 0)),
        scratch_shapes=[pltpu.VMEM((d, f), BF16), pltpu.VMEM((d, f), BF16), pltpu.VMEM((f, d), BF16)],
    )
    return pl.pallas_call(
        _experts_kernel,
        grid_spec=grid_spec,
        out_shape=jax.ShapeDtypeStruct((rows, d), BF16),
        compiler_params=_cparams(("arbitrary",)),
        name="experts",
    )(block_e, n_used, buf, w1, w3, w2)


def _combine_kernel(x_ref, y0_ref, y1_ref, w_ref, g2_ref, fg_ref, o_ref, *, final):
    w = w_ref[...]
    y = y0_ref[...].astype(F32) * w[:, 0:1] + y1_ref[...].astype(F32) * w[:, 1:2]
    x = x_ref[...] + g2_ref[...] * y
    if final:
        r = lax.rsqrt(jnp.mean(x * x, axis=-1, keepdims=True) + EPS)
        x = x * r * fg_ref[...]
    o_ref[...] = x


def moe_combine(x, y0, y1, wts, mods, final_g, final, tm=512):
    m, d = x.shape
    tm = min(tm, mods.rows_per_mod)
    row = lambda w: pl.BlockSpec((tm, w), lambda i: (i, 0))
    return pl.pallas_call(
        functools.partial(_combine_kernel, final=final),
        grid=(m // tm,),
        in_specs=[row(d), row(d), row(d), row(TOP_K), mods.spec(5, tm),
                  pl.BlockSpec((1, d), lambda i: (0, 0))],
        out_specs=row(d),
        out_shape=jax.ShapeDtypeStruct((m, d), F32),
        compiler_params=_cparams(("parallel",)),
        name="moe_combine",
    )(x, y0, y1, wts, mods.mod, final_g)


def moe_dispatch(ri, cnt):
    t = ri.shape[1]
    counts = cnt[:, 0].astype(jnp.int32).reshape(EXPERTS_PER_GROUP, N_GROUPS).T.reshape(N_EXPERTS)
    padded = (counts + MOE_ROWS - 1) // MOE_ROWS * MOE_ROWS
    ends_p = jnp.cumsum(padded)
    starts_p = ends_p - padded
    dest = starts_p[ri[0:2]] + ri[2:4]
    n_blocks = -(-(t * TOP_K) // MOE_ROWS) + N_EXPERTS
    tok = jnp.arange(t, dtype=jnp.int32)
    src_tok = jnp.zeros((n_blocks * MOE_ROWS,), jnp.int32).at[dest.reshape(-1)].set(jnp.concatenate([tok, tok]))
    block_start = jnp.arange(n_blocks, dtype=jnp.int32) * MOE_ROWS
    block_e = jnp.minimum(jnp.sum(ends_p[None, :] <= block_start[:, None], axis=1), N_EXPERTS - 1).astype(jnp.int32)
    n_used = (ends_p[-1:] // MOE_ROWS).astype(jnp.int32)
    return src_tok, dest, block_e, n_used


def rope_tables(n_tok):
    t = jnp.arange(n_tok)
    row = (t // GRID_W).astype(F32)
    col = (t % GRID_W).astype(F32)
    quarter = DH_A // 4
    inv = ROPE_BASE ** (-jnp.arange(quarter, dtype=F32) / quarter)
    ang = jnp.concatenate([row[:, None] * inv, col[:, None] * inv], axis=-1)
    cos, sin = jnp.cos(ang), jnp.sin(ang)
    cos128 = jnp.tile(jnp.concatenate([cos, cos], axis=-1), (1, 2))
    sin128 = jnp.tile(jnp.concatenate([-sin, sin], axis=-1), (1, 2))
    return cos128, sin128


def trunk_layer(x, mods, pw, layer, ctx, new_caches, final):
    b, n, d = x.shape
    m = b * n
    xf = x.reshape(m, d)
    h = norm_mod(xf, pw['norm1_g'], mods, layer)
    slab = matmul(h, pw['w_in'], layer, N_BRANCH * D_MODEL, N_QKV, F32, tn=256, name="in_proj_qkv")
    gates = matmul(h, pw['w_in'], layer, 0, N_BRANCH * D_MODEL, BF16, act="sigmoid", tn=512, name="in_proj_gate")
    slab3 = slab.reshape(b, n, N_QKV)
    lam, lam_init = pw['lam'], pw['lam_init'][layer]
    if ctx is None:
        o_a, o_c, o_d, new_caches = ctx_mixers(slab3, lam, lam_init, pw['diff_norm_g'], pw['sink_c'], layer,
                                               new_caches)
        s0 = jnp.zeros((b, 2, G_B, P_B), F32)
        y_ssm, fin_r, fin_i = ssm_scan(slab3, pw['ssm_mats'], layer, s0, s0)
        new_ctx = (new_caches, fin_r, fin_i)
    else:
        ak, av, s_re, s_im, ck, cv, dk, dv = ctx
        a_qkv, c_qkv, d_qkv = lat_prep(slab3, pw['cos128'], pw['sin128'])
        o_a = lat_attn_a(a_qkv, ak, av, layer, lam, lam_init, pw['diff_norm_g'])
        o_c = lat_attn_c(c_qkv, ck, cv, layer, pw['sink_c'])
        o_d = lat_attn_d(d_qkv, dk, dv, layer, pw['bias_d'])
        y_ssm, _, _ = ssm_scan(slab3, pw['ssm_mats'], layer, s_re[:, layer], s_im[:, layer])
        new_ctx = None
    merged = branch_merge(o_a.reshape(m, -1), o_c.reshape(m, -1), o_d.reshape(m, -1), slab,
                          [t.reshape(m, LANES) for t in y_ssm], gates, pw['ssm_d'], pw['w_glu'], pw['w_branch'],
                          layer)
    x1, h2, ri, rw, cnt = out_proj(xf, merged, pw['w_out'], pw['norm2_g'], mods, layer,
                                   pw['router_wt'], pw['router_bt'])
    src_tok, dest, block_e, n_used = moe_dispatch(ri, cnt)
    yb = experts(h2[src_tok], block_e, n_used, pw['w_e1'], pw['w_e3'], pw['w_e2'], layer)
    x2 = moe_combine(x1, yb[dest[0]], yb[dest[1]], rw[0:2].T, mods, pw['final_g'], final)
    return x2.reshape(b, n, d), new_ctx


def kernel(x_prompt, x_sample, cache_a_k, cache_a_v, state_ssm_re, state_ssm_im, cache_c_k, cache_c_v, cache_d_k, cache_d_v, c, c_ctx, w_mod, b_mod, norm1_g, norm2_g, w_in, diff_lambda, diff_norm_g, ssm_lam_re, ssm_lam_im, ssm_log_dt, ssm_b_re, ssm_b_im, ssm_c_re, ssm_c_im, ssm_d, ssm_w_glu, sink_c, rpb_d, w_branch, w_out, router_w, router_b, w_e1, w_e3, w_e2, final_g):
    depth = w_in.shape[0]
    db, dn, d = x_sample.shape
    past = cache_a_k.shape[2]
    cond = jnp.concatenate([c_ctx[None, :], c], axis=0).astype(F32)
    n_cond = cond.shape[0]
    cond = jnp.pad(cond, ((0, -n_cond % 8), (0, 0)))
    mod_all = adaln_all(cond, w_mod, b_mod[:, None, :])
    mod_all = mod_all.reshape(depth, cond.shape[0], 1, 6 * d)
    cos128, sin128 = rope_tables(dn)
    perm = (np.arange(N_GROUPS)[None, :] * EXPERTS_PER_GROUP + np.arange(EXPERTS_PER_GROUP)[:, None]).reshape(-1)
    router_wt = router_w.astype(F32).T[perm]
    router_bt = router_b.astype(F32)[perm][:, None]
    ctx_caches = (cache_a_k.reshape(db, depth, past, H_A * 2 * DH_A), cache_a_v.reshape(db, depth, past, H_A * DV_A),
                  state_ssm_re, state_ssm_im,
                  cache_c_k.reshape(db, depth, past, KVH_C * DH_C), cache_c_v.reshape(db, depth, past, KVH_C * DH_C),
                  cache_d_k.reshape(db, depth, past, H_D * DH_D), cache_d_v.reshape(db, depth, past, H_D * DH_D))
    lam_init = [0.8 - 0.6 * math.exp(-0.3 * l) for l in range(depth)]
    dl = diff_lambda.astype(F32)
    lam = (jnp.exp(jnp.sum(dl[:, 0] * dl[:, 1], axis=-1)) - jnp.exp(jnp.sum(dl[:, 2] * dl[:, 3], axis=-1))
           + jnp.asarray(lam_init, F32)).reshape(depth, 1, 1)
    row3 = lambda a: a.astype(F32)[:, None, :]
    pw = {
        'norm1_g': row3(norm1_g), 'norm2_g': row3(norm2_g), 'final_g': final_g.astype(F32)[None, :],
        'w_in': jnp.concatenate([w_in[:, :, N_QKV:], w_in[:, :, :N_QKV]], axis=2).astype(BF16),
        'lam': lam, 'lam_init': lam_init, 'diff_norm_g': row3(diff_norm_g), 'sink_c': row3(sink_c),
        'ssm_mats': ssm_matrices(ssm_lam_re, ssm_lam_im, ssm_log_dt, ssm_b_re, ssm_b_im, ssm_c_re, ssm_c_im),
        'ssm_d': row3(ssm_d), 'w_glu': ssm_w_glu.astype(BF16),
        'w_branch': w_branch.astype(BF16), 'w_out': w_out.astype(BF16),
        'router_wt': router_wt, 'router_bt': router_bt,
        'w_e1': w_e1, 'w_e3': w_e3, 'w_e2': w_e2,
        'cos128': cos128, 'sin128': sin128,
        'bias_d': neighbourhood_bias(rpb_d, dn // GRID_W),
    }
    pb, pn, _ = x_prompt.shape
    new_caches = tuple(jnp.zeros((pb, depth, pn, c1 - c0), F32) for c0, c1 in CACHE_COLS)
    y_p, y_s = x_prompt, x_sample
    fins_r, fins_i = [], []
    for l in range(depth):
        final = l == depth - 1
        y_p, (new_caches, fin_r, fin_i) = trunk_layer(y_p, ModRows(mod_all, l, 0, pb * pn), pw, l, None, new_caches,
                                                      final)
        y_s, _ = trunk_layer(y_s, ModRows(mod_all, l, 1, dn), pw, l, ctx_caches, None, final)
        fins_r.append(fin_r)
        fins_i.append(fin_i)
    a_k, a_v, c_k, c_v, d_k, d_v = new_caches
    return (y_p, y_s,
            a_k.reshape(pb, depth, pn, H_A, 2, DH_A), a_v.reshape(pb, depth, pn, H_A, DV_A),
            jnp.stack(fins_r, axis=1), jnp.stack(fins_i, axis=1),
            c_k.reshape(pb, depth, pn, KVH_C, DH_C), c_v.reshape(pb, depth, pn, KVH_C, DH_C),
            d_k.reshape(pb, depth, pn, H_D, DH_D), d_v.reshape(pb, depth, pn, H_D, DH_D))
```
